```python
import math
import jax, jax.numpy as jnp
from jax import lax
import numpy as np

D_MODEL = 2048
BATCH = 2
SEQ = 4096
DEPTH = 1

MIX_WIDTH = D_MODEL
GROUP_DIM = 128
CONV_WIDTH = MIX_WIDTH // 2
GM_WIDTH = MIX_WIDTH - CONV_WIDTH
CONV_GROUPS = CONV_WIDTH // GROUP_DIM
GM_HEADS = GM_WIDTH // GROUP_DIM
CONV_K = 3
CHUNK = 128
D_FF = 256 * math.ceil(8 * D_MODEL / 3 / 256)
N_MEM = 256
XA_HEADS = 4
XA_HEAD_DIM = D_MODEL // XA_HEADS
IN_COLS = 3 * CONV_WIDTH + 2 * GM_WIDTH
EPS = 1e-6

kernel_name = "hybrid_conv_gmlp_macaron_xattn"


def _rmsnorm(x, g):
    x32 = x.astype(jnp.float32)
    y = x32 * lax.rsqrt(jnp.mean(x32 * x32, axis=-1, keepdims=True) + EPS)
    return (y * g.astype(jnp.float32)).astype(x.dtype)


def _swiglu(x, w_in, w_out):
    gate, up = jnp.split(x @ w_in, 2, axis=-1)
    return (jax.nn.silu(gate) * up) @ w_out


def _causal_dwconv(z, conv_w, conv_b):
    s = z.shape[1]
    zp = jnp.pad(z, ((0, 0), (CONV_K - 1, 0), (0, 0)))
    y = conv_b + conv_w[CONV_K - 1] * z
    for k in range(CONV_K - 1):
        y = y + conv_w[k] * zp[:, k:k + s]
    return y


def _spatial_gating(u, v, g_v, w_s, b_s):
    b, s, _ = v.shape
    vh = v.reshape(b, s // CHUNK, CHUNK, GM_HEADS, GROUP_DIM)
    vh = _rmsnorm(vh, g_v.reshape(GM_HEADS, GROUP_DIM))
    mask = jnp.tril(jnp.ones((CHUNK, CHUNK), dtype=bool))
    w = jnp.where(mask[None], w_s, jnp.zeros_like(w_s)).astype(vh.dtype)
    sg = jnp.einsum('hts,bnshc->bnthc', w, vh) + b_s.T[None, None, :, :, None].astype(vh.dtype)
    return u * sg.reshape(b, s, GM_WIDTH)


def _cross_attention(n, mem_n, w_q, w_k, w_v, w_o):
    b, s, _ = n.shape
    m = mem_n.shape[1]
    q = (n @ w_q).reshape(b, s, XA_HEADS, XA_HEAD_DIM)
    k = (mem_n @ w_k).reshape(b, m, XA_HEADS, XA_HEAD_DIM)
    v = (mem_n @ w_v).reshape(b, m, XA_HEADS, XA_HEAD_DIM)
    scores = jnp.einsum('bshd,bmhd->bhsm', q, k).astype(jnp.float32) * (XA_HEAD_DIM ** -0.5)
    p = jax.nn.softmax(scores, axis=-1).astype(q.dtype)
    o = jnp.einsum('bhsm,bmhd->bshd', p, v).reshape(b, s, XA_HEADS * XA_HEAD_DIM)
    return o @ w_o


def setup_inputs(seed: int = 0) -> dict:
    key = jax.random.key(seed)
    ks = jax.random.split(key, 32)
    f32 = jnp.float32

    def w(k, shape, fan_in):
        return jax.random.normal(k, shape, f32) * (fan_in ** -0.5)

    def gain(k, shape):
        return 1.0 + 0.1 * jax.random.normal(k, shape, f32)

    L = DEPTH
    return {
        "x": jax.random.normal(ks[0], (BATCH, SEQ, D_MODEL), f32),
        "mem": jax.random.normal(ks[1], (BATCH, N_MEM, D_MODEL), f32),
        "g_ffn1": gain(ks[2], (L, D_MODEL)),
        "w_ffn1_in": w(ks[3], (L, D_MODEL, 2 * D_FF), D_MODEL),
        "w_ffn1_out": w(ks[4], (L, D_FF, D_MODEL), D_FF),
        "g_mix": gain(ks[5], (L, D_MODEL)),
        "w_mix_in": w(ks[6], (L, D_MODEL, IN_COLS), D_MODEL),
        "conv_w": w(ks[7], (L, CONV_K, CONV_WIDTH), CONV_K),
        "conv_b": 0.01 * jax.random.normal(ks[8], (L, CONV_WIDTH), f32),
        "g_gm_v": gain(ks[9], (L, GM_WIDTH)),
        "w_spatial": w(ks[10], (L, GM_HEADS, CHUNK, CHUNK), CHUNK),
        "b_spatial": 1.0 + 0.1 * jax.random.normal(ks[11], (L, GM_HEADS, CHUNK), f32),
        "w_mix_out": w(ks[12], (L, MIX_WIDTH, D_MODEL), MIX_WIDTH),
        "g_xattn": gain(ks[13], (L, D_MODEL)),
        "g_mem": gain(ks[14], (L, D_MODEL)),
        "w_xq": w(ks[15], (L, D_MODEL, D_MODEL), D_MODEL),
        "w_xk": w(ks[16], (L, D_MODEL, D_MODEL), D_MODEL),
        "w_xv": w(ks[17], (L, D_MODEL, D_MODEL), D_MODEL),
        "w_xo": w(ks[18], (L, D_MODEL, D_MODEL), D_MODEL),
        "g_ffn2": gain(ks[19], (L, D_MODEL)),
        "w_ffn2_in": w(ks[20], (L, D_MODEL, 2 * D_FF), D_MODEL),
        "w_ffn2_out": w(ks[21], (L, D_FF, D_MODEL), D_FF),
        "g_final": gain(ks[22], (D_MODEL,)),
    }


def reference(x, mem, g_ffn1, w_ffn1_in, w_ffn1_out, g_mix, w_mix_in, conv_w, conv_b,
              g_gm_v, w_spatial, b_spatial, w_mix_out, g_xattn, g_mem, w_xq, w_xk, w_xv,
              w_xo, g_ffn2, w_ffn2_in, w_ffn2_out, g_final):
    h = x
    splits = [CONV_WIDTH, 2 * CONV_WIDTH, 3 * CONV_WIDTH, 3 * CONV_WIDTH + GM_WIDTH]
    for l in range(DEPTH):
        h = h + 0.5 * _swiglu(_rmsnorm(h, g_ffn1[l]), w_ffn1_in[l], w_ffn1_out[l])

        z = _rmsnorm(h, g_mix[l]) @ w_mix_in[l]
        gate_b, gate_c, h_c, u, v = jnp.split(z, splits, axis=-1)
        y_conv = gate_b * _causal_dwconv(gate_c * h_c, conv_w[l], conv_b[l])
        y_gm = _spatial_gating(u, v, g_gm_v[l], w_spatial[l], b_spatial[l])
        h = h + jnp.concatenate([y_conv, y_gm], axis=-1) @ w_mix_out[l]

        h = h + _cross_attention(_rmsnorm(h, g_xattn[l]), _rmsnorm(mem, g_mem[l]),
                                 w_xq[l], w_xk[l], w_xv[l], w_xo[l])

        h = h + 0.5 * _swiglu(_rmsnorm(h, g_ffn2[l]), w_ffn2_in[l], w_ffn2_out[l])
    return _rmsnorm(h, g_final)
```

```python
import functools
import math

import jax
import jax.numpy as jnp
from jax import lax
from jax.experimental import pallas as pl
from jax.experimental.pallas import tpu as pltpu

GROUP_DIM = 128
CHUNK = 128
CONV_K = 3
XA_HEADS = 4
EPS = 1e-6

V7X_VMEM_BYTES = 64 * 1024 * 1024
SUBLANES = 8
MIX_COLS = 256

BF16 = jnp.bfloat16
F32 = jnp.float32


def _rmsnorm(x, g):
    y = x * lax.rsqrt(jnp.mean(x * x, axis=-1, keepdims=True) + EPS)
    return y * g


def _dot(a, b):
    return jnp.dot(a, b, preferred_element_type=F32)


def _vmem_limit(nbytes):
    return min(int(nbytes * 1.25) + (8 << 20), V7X_VMEM_BYTES - (4 << 20))


def _ffn_kernel(x_ref, g_ref, wg_ref, wu_ref, wo_ref, gf_ref, o_ref, xn_ref, acc_ref,
                *, final_norm):
    k = pl.program_id(1)

    @pl.when(k == 0)
    def _():
        xn_ref[...] = _rmsnorm(x_ref[...], g_ref[...]).astype(BF16)
        acc_ref[...] = jnp.zeros_like(acc_ref)

    xn = xn_ref[...]
    gate = _dot(xn, wg_ref[...])
    up = _dot(xn, wu_ref[...])
    hidden = (gate * jax.nn.sigmoid(gate) * up).astype(BF16)
    acc_ref[...] += _dot(hidden, wo_ref[...])

    @pl.when(k == pl.num_programs(1) - 1)
    def _():
        h = x_ref[...] + 0.5 * acc_ref[...]
        if final_norm:
            h = _rmsnorm(h, gf_ref[...])
        o_ref[...] = h


def _ffn(x, g, w_in, w_out, g_final, *, final_norm, tm, tf):
    m, d = x.shape
    d_ff = w_out.shape[0]
    nk = d_ff // tf
    assert m % tm == 0 and d_ff % tf == 0
    vmem = (2 * 2 * tm * d * 4
            + tm * d * (2 + 4)
            + 2 * 3 * d * tf * 2)
    return pl.pallas_call(
        functools.partial(_ffn_kernel, final_norm=final_norm),
        grid=(m // tm, nk),
        in_specs=[
            pl.BlockSpec((tm, d), lambda i, k: (i, 0)),
            pl.BlockSpec((1, d), lambda i, k: (0, 0)),
            pl.BlockSpec((d, tf), lambda i, k: (0, k)),
            pl.BlockSpec((d, tf), lambda i, k: (0, nk + k)),
            pl.BlockSpec((tf, d), lambda i, k: (k, 0)),
            pl.BlockSpec((1, d), lambda i, k: (0, 0)),
        ],
        out_specs=pl.BlockSpec((tm, d), lambda i, k: (i, 0)),
        out_shape=jax.ShapeDtypeStruct((m, d), F32),
        scratch_shapes=[pltpu.VMEM((tm, d), BF16), pltpu.VMEM((tm, d), F32)],
        compiler_params=pltpu.CompilerParams(
            dimension_semantics=("arbitrary", "arbitrary"),
            vmem_limit_bytes=_vmem_limit(vmem)),
        name="ffn_final" if final_norm else "ffn",
    )(x, g, w_in, w_in, w_out, g_final)


def _mix_kernel(h_ref, g_ref, wb_ref, wc_ref, wh_ref, wu_ref, wv_ref, cw_ref, cb_ref,
                gv_ref, ws_ref, bs_ref, woc_ref, wog_ref, o_ref, hn_ref, ygm_ref, carry_ref,
                *, tiles_per_seq):
    i = pl.program_id(0)
    j = pl.program_id(1)
    tm = h_ref.shape[0]

    @pl.when(j == 0)
    def _():
        hn_ref[...] = _rmsnorm(h_ref[...], g_ref[...]).astype(BF16)
        o_ref[...] = h_ref[...]

    @pl.when(i % tiles_per_seq == 0)
    def _():
        carry_ref[j] = jnp.zeros((SUBLANES, MIX_COLS), F32)

    hn = hn_ref[...]

    z = _dot(hn, wc_ref[...]) * _dot(hn, wh_ref[...])
    gate_b = _dot(hn, wb_ref[...])
    prev = carry_ref[j]
    carry_ref[j] = z[tm - SUBLANES:, :]
    cw = cw_ref[...]
    cb = cb_ref[...]
    row = lax.broadcasted_iota(jnp.int32, (SUBLANES, MIX_COLS), 0)
    z1 = pltpu.roll(z, 1, 0)
    z2 = pltpu.roll(z, 2, 0)
    head1 = jnp.where(row < 1, pltpu.roll(prev, 1, 0), z1[:SUBLANES])
    head2 = jnp.where(row < 2, pltpu.roll(prev, 2, 0), z2[:SUBLANES])
    z1 = jnp.concatenate([head1, z1[SUBLANES:]], axis=0)
    z2 = jnp.concatenate([head2, z2[SUBLANES:]], axis=0)
    y_conv = gate_b * (cb + cw[2:3] * z + cw[1:2] * z1 + cw[0:1] * z2)
    o_ref[...] += _dot(y_conv.astype(BF16), woc_ref[...])

    u = _dot(hn, wu_ref[...])
    v = _dot(hn, wv_ref[...])
    gv = gv_ref[...]
    bs = bs_ref[...]
    tri = (lax.broadcasted_iota(jnp.int32, (CHUNK, CHUNK), 0)
           >= lax.broadcasted_iota(jnp.int32, (CHUNK, CHUNK), 1))
    for hh in range(MIX_COLS // GROUP_DIM):
        lanes = slice(hh * GROUP_DIM, (hh + 1) * GROUP_DIM)
        vh = _rmsnorm(v[:, lanes], gv[:, lanes]).astype(BF16)
        w = jnp.where(tri, ws_ref[hh], 0.0).astype(BF16)
        for c in range(tm // CHUNK):
            rows = slice(c * CHUNK, (c + 1) * CHUNK)
            sg = _dot(w, vh[rows]) + bs[:, hh:hh + 1]
            ygm_ref[rows, lanes] = (u[rows, lanes] * sg).astype(BF16)
    o_ref[...] += _dot(ygm_ref[...], wog_ref[...])


def _mix(h, g, w_in, conv_w, conv_b, g_v, w_s, b_s_t, w_out, *, seq, tm):
    m, d = h.shape
    width = w_out.shape[0]
    conv_width = width // 2
    nj = conv_width // MIX_COLS
    heads_per_step = MIX_COLS // GROUP_DIM
    assert m % tm == 0 and seq % tm == 0 and tm % CHUNK == 0

    def col_block(offset):
        return pl.BlockSpec((d, MIX_COLS), lambda i, j: (0, offset + j))

    vmem = (2 * 2 * tm * d * 4 + tm * d * 2 + tm * MIX_COLS * 2
            + 2 * 5 * d * MIX_COLS * 2 + 2 * 2 * MIX_COLS * d * 2)
    return pl.pallas_call(
        functools.partial(_mix_kernel, tiles_per_seq=seq // tm),
        grid=(m // tm, nj),
        in_specs=[
            pl.BlockSpec((tm, d), lambda i, j: (i, 0)),
            pl.BlockSpec((1, d), lambda i, j: (0, 0)),
            col_block(0), col_block(nj), col_block(2 * nj),
            col_block(3 * nj), col_block(4 * nj),
            pl.BlockSpec((CONV_K, MIX_COLS), lambda i, j: (0, j)),
            pl.BlockSpec((1, MIX_COLS), lambda i, j: (0, j)),
            pl.BlockSpec((1, MIX_COLS), lambda i, j: (0, j)),
            pl.BlockSpec((heads_per_step, CHUNK, CHUNK), lambda i, j: (j, 0, 0)),
            pl.BlockSpec((None, CHUNK, heads_per_step), lambda i, j: (j, 0, 0)),
            pl.BlockSpec((MIX_COLS, d), lambda i, j: (j, 0)),
            pl.BlockSpec((MIX_COLS, d), lambda i, j: (nj + j, 0)),
        ],
        out_specs=pl.BlockSpec((tm, d), lambda i, j: (i, 0)),
        out_shape=jax.ShapeDtypeStruct((m, d), F32),
        scratch_shapes=[pltpu.VMEM((tm, d), BF16), pltpu.VMEM((tm, MIX_COLS), BF16),
                        pltpu.VMEM((nj, SUBLANES, MIX_COLS), F32)],
        compiler_params=pltpu.CompilerParams(
            dimension_semantics=("arbitrary", "arbitrary"),
            vmem_limit_bytes=_vmem_limit(vmem)),
        name="mix",
    )(h, g, w_in, w_in, w_in, w_in, w_in, conv_w, conv_b, g_v, w_s, b_s_t, w_out, w_out)


def _kv_kernel(mem_ref, g_ref, wk_ref, wv_ref, k_ref, v_ref, mn_ref):
    @pl.when(pl.program_id(0) == 0)
    def _():
        mn_ref[...] = _rmsnorm(mem_ref[...], g_ref[...]).astype(BF16)

    mn = mn_ref[...]
    k_ref[...] = _dot(mn, wk_ref[...]).astype(BF16)
    v_ref[...] = _dot(mn, wv_ref[...]).astype(BF16)


def _kv(mem, g, w_k, w_v, *, tn):
    m, d = mem.shape
    vmem = 2 * m * d * 4 + m * d * 2 + 2 * 2 * d * tn * 2 + 2 * 2 * m * tn * 2
    return pl.pallas_call(
        _kv_kernel,
        grid=(d // tn,),
        in_specs=[
            pl.BlockSpec((m, d), lambda n: (0, 0)),
            pl.BlockSpec((1, d), lambda n: (0, 0)),
            pl.BlockSpec((d, tn), lambda n: (0, n)),
            pl.BlockSpec((d, tn), lambda n: (0, n)),
        ],
        out_specs=[pl.BlockSpec((m, tn), lambda n: (0, n))] * 2,
        out_shape=[jax.ShapeDtypeStruct((m, d), BF16)] * 2,
        scratch_shapes=[pltpu.VMEM((m, d), BF16)],
        compiler_params=pltpu.CompilerParams(
            dimension_semantics=("arbitrary",), vmem_limit_bytes=_vmem_limit(vmem)),
        name="kv",
    )(mem, g, w_k, w_v)


def _xattn_kernel(h_ref, g_ref, wq_ref, k_ref, v_ref, wo_ref, o_ref, hn_ref, acc_ref):
    hd = pl.program_id(1)

    @pl.when(hd == 0)
    def _():
        hn_ref[...] = _rmsnorm(h_ref[...], g_ref[...]).astype(BF16)
        acc_ref[...] = jnp.zeros_like(acc_ref)

    q = _dot(hn_ref[...], wq_ref[...])
    scores = lax.dot_general(q.astype(BF16), k_ref[...], (((1,), (1,)), ((), ())),
                             preferred_element_type=F32)
    scores = scores * (q.shape[-1] ** -0.5)
    p = jnp.exp(scores - jnp.max(scores, axis=-1, keepdims=True))
    p = p / jnp.sum(p, axis=-1, keepdims=True)
    o = _dot(p.astype(BF16), v_ref[...])
    acc_ref[...] += _dot(o.astype(BF16), wo_ref[...])

    @pl.when(hd == pl.num_programs(1) - 1)
    def _():
        o_ref[...] = h_ref[...] + acc_ref[...]


def _xattn(h, g, w_q, k, v, w_o, *, seq, n_mem, tm):
    m, d = h.shape
    hdim = d // XA_HEADS
    tiles_per_seq = seq // tm
    assert m % tm == 0 and seq % tm == 0
    vmem = (2 * 2 * tm * d * 4 + tm * d * (2 + 4)
            + 2 * 2 * d * hdim * 2 + 2 * 2 * n_mem * hdim * 2)
    return pl.pallas_call(
        _xattn_kernel,
        grid=(m // tm, XA_HEADS),
        in_specs=[
            pl.BlockSpec((tm, d), lambda i, hd: (i, 0)),
            pl.BlockSpec((1, d), lambda i, hd: (0, 0)),
            pl.BlockSpec((d, hdim), lambda i, hd: (0, hd)),
            pl.BlockSpec((n_mem, hdim), lambda i, hd: (i // tiles_per_seq, hd)),
            pl.BlockSpec((n_mem, hdim), lambda i, hd: (i // tiles_per_seq, hd)),
            pl.BlockSpec((hdim, d), lambda i, hd: (hd, 0)),
        ],
        out_specs=pl.BlockSpec((tm, d), lambda i, hd: (i, 0)),
        out_shape=jax.ShapeDtypeStruct((m, d), F32),
        scratch_shapes=[pltpu.VMEM((tm, d), BF16), pltpu.VMEM((tm, d), F32)],
        compiler_params=pltpu.CompilerParams(
            dimension_semantics=("arbitrary", "arbitrary"),
            vmem_limit_bytes=_vmem_limit(vmem)),
        name="xattn",
    )(h, g, w_q, k, v, w_o)


def kernel(x, mem, g_ffn1, w_ffn1_in, w_ffn1_out, g_mix, w_mix_in, conv_w, conv_b, g_gm_v,
           w_spatial, b_spatial, w_mix_out, g_xattn, g_mem, w_xq, w_xk, w_xv, w_xo, g_ffn2,
           w_ffn2_in, w_ffn2_out, g_final):
    b, s, d = x.shape
    n_mem = mem.shape[1]
    depth = g_ffn1.shape[0]
    assert depth >= 1
    heads_per_step = MIX_COLS // GROUP_DIM
    tm = 512
    tf = 512

    def row(v):
        return v.reshape(1, -1)

    def wt(w):
        return w.astype(BF16)

    h = x.reshape(b * s, d)
    mem2 = mem.reshape(b * n_mem, d)
    g_fin = row(g_final)
    for l in range(depth):
        last = l == depth - 1
        h = _ffn(h, row(g_ffn1[l]), wt(w_ffn1_in[l]), wt(w_ffn1_out[l]), g_fin,
                 final_norm=False, tm=tm, tf=tf)
        b_s_t = b_spatial[l].reshape(-1, heads_per_step, CHUNK).transpose(0, 2, 1)
        h = _mix(h, row(g_mix[l]), wt(w_mix_in[l]), conv_w[l], row(conv_b[l]), row(g_gm_v[l]),
                 w_spatial[l], b_s_t, wt(w_mix_out[l]), seq=s, tm=tm)
        k, v = _kv(mem2, row(g_mem[l]), wt(w_xk[l]), wt(w_xv[l]), tn=512)
        h = _xattn(h, row(g_xattn[l]), wt(w_xq[l]), k, v, wt(w_xo[l]), seq=s, n_mem=n_mem, tm=tm)
        h = _ffn(h, row(g_ffn2[l]), wt(w_ffn2_in[l]), wt(w_ffn2_out[l]), g_fin,
                 final_norm=last, tm=tm, tf=tf)
    return h.reshape(b, s, d)
```

```python
import functools
import math

import jax
import jax.numpy as jnp
from jax import lax
from jax.experimental import pallas as pl
from jax.experimental.pallas import tpu as pltpu

GROUP_DIM = 128
CHUNK = 128
CONV_K = 3
XA_HEADS = 4
EPS = 1e-6

V7X_VMEM_BYTES = 64 * 1024 * 1024
SUBLANES = 8
MIX_COLS = 256

BF16 = jnp.bfloat16
F32 = jnp.float32


def _rmsnorm(x, g):
    y = x * lax.rsqrt(jnp.mean(x * x, axis=-1, keepdims=True) + EPS)
    return y * g


def _dot(a, b):
    return jnp.dot(a, b, preferred_element_type=F32)


def _vmem_limit(nbytes):
    return min(int(nbytes * 1.25) + (8 << 20), V7X_VMEM_BYTES - (4 << 20))


def _ffn_kernel(x_ref, g_ref, wg_ref, wu_ref, wo_ref, gf_ref, o_ref, xn_ref, *, final_norm):
    k = pl.program_id(1)

    @pl.when(k == 0)
    def _():
        x = x_ref[...]
        xn_ref[...] = _rmsnorm(x, g_ref[...]).astype(BF16)
        o_ref[...] = x

    xn = xn_ref[...]
    gate = _dot(xn, wg_ref[...].astype(BF16))
    up = _dot(xn, wu_ref[...].astype(BF16))
    hidden = (0.5 * gate * jax.nn.sigmoid(gate) * up).astype(BF16)
    o_ref[...] += _dot(hidden, wo_ref[...].astype(BF16))

    if final_norm:
        @pl.when(k == pl.num_programs(1) - 1)
        def _():
            o_ref[...] = _rmsnorm(o_ref[...], gf_ref[...])


def _ffn(x, g, w_in, w_out, g_final, *, final_norm, tm, tf):
    m, d = x.shape
    d_ff = w_out.shape[0]
    nk = d_ff // tf
    assert m % tm == 0 and d_ff % tf == 0
    vmem = (2 * 2 * tm * d * 4
            + tm * d * 2
            + 2 * 3 * d * tf * w_in.dtype.itemsize)
    return pl.pallas_call(
        functools.partial(_ffn_kernel, final_norm=final_norm),
        grid=(m // tm, nk),
        in_specs=[
            pl.BlockSpec((tm, d), lambda i, k: (i, 0)),
            pl.BlockSpec((1, d), lambda i, k: (0, 0)),
            pl.BlockSpec((d, tf), lambda i, k: (0, k)),
            pl.BlockSpec((d, tf), lambda i, k: (0, nk + k)),
            pl.BlockSpec((tf, d), lambda i, k: (k, 0)),
            pl.BlockSpec((1, d), lambda i, k: (0, 0)),
        ],
        out_specs=pl.BlockSpec((tm, d), lambda i, k: (i, 0)),
        out_shape=jax.ShapeDtypeStruct((m, d), F32),
        scratch_shapes=[pltpu.VMEM((tm, d), BF16)],
        compiler_params=pltpu.CompilerParams(
            dimension_semantics=("arbitrary", "arbitrary"),
            vmem_limit_bytes=_vmem_limit(vmem)),
        name="ffn_final" if final_norm else "ffn",
    )(x, g, w_in, w_in, w_out, g_final)


def _mix_kernel(h_ref, g_ref, wb_ref, wc_ref, wh_ref, wu_ref, wv_ref, cw_ref, cb_ref,
                gv_ref, ws_ref, bs_ref, woc_ref, wog_ref, o_ref, hn_ref, ygm_ref, carry_ref,
                *, tiles_per_seq):
    i = pl.program_id(0)
    j = pl.program_id(1)
    tm = h_ref.shape[0]

    @pl.when(j == 0)
    def _():
        hn_ref[...] = _rmsnorm(h_ref[...], g_ref[...]).astype(BF16)
        o_ref[...] = h_ref[...]

    @pl.when(i % tiles_per_seq == 0)
    def _():
        carry_ref[j] = jnp.zeros((SUBLANES, MIX_COLS), F32)

    hn = hn_ref[...]

    z = _dot(hn, wc_ref[...].astype(BF16)) * _dot(hn, wh_ref[...].astype(BF16))
    gate_b = _dot(hn, wb_ref[...].astype(BF16))
    prev = carry_ref[j]
    carry_ref[j] = z[tm - SUBLANES:, :]
    cw = cw_ref[...]
    cb = cb_ref[...]
    row = lax.broadcasted_iota(jnp.int32, (SUBLANES, MIX_COLS), 0)
    z1 = pltpu.roll(z, 1, 0)
    z2 = pltpu.roll(z, 2, 0)
    head1 = jnp.where(row < 1, pltpu.roll(prev, 1, 0), z1[:SUBLANES])
    head2 = jnp.where(row < 2, pltpu.roll(prev, 2, 0), z2[:SUBLANES])
    z1 = jnp.concatenate([head1, z1[SUBLANES:]], axis=0)
    z2 = jnp.concatenate([head2, z2[SUBLANES:]], axis=0)
    y_conv = gate_b * (cb + cw[2:3] * z + cw[1:2] * z1 + cw[0:1] * z2)
    o_ref[...] += _dot(y_conv.astype(BF16), woc_ref[...].astype(BF16))

    u = _dot(hn, wu_ref[...].astype(BF16))
    v = _dot(hn, wv_ref[...].astype(BF16))
    gv = gv_ref[...]
    bs = bs_ref[...]
    tri = (lax.broadcasted_iota(jnp.int32, (CHUNK, CHUNK), 0)
           >= lax.broadcasted_iota(jnp.int32, (CHUNK, CHUNK), 1))
    for hh in range(MIX_COLS // GROUP_DIM):
        lanes = slice(hh * GROUP_DIM, (hh + 1) * GROUP_DIM)
        vh = _rmsnorm(v[:, lanes], gv[:, lanes]).astype(BF16)
        w = jnp.where(tri, ws_ref[hh], 0.0).astype(BF16)
        for c in range(tm // CHUNK):
            rows = slice(c * CHUNK, (c + 1) * CHUNK)
            sg = _dot(w, vh[rows]) + bs[:, hh:hh + 1]
            ygm_ref[rows, lanes] = (u[rows, lanes] * sg).astype(BF16)
    o_ref[...] += _dot(ygm_ref[...], wog_ref[...].astype(BF16))


def _mix(h, g, w_in, conv_w, conv_b, g_v, w_s, b_s_t, w_out, *, seq, tm):
    m, d = h.shape
    width = w_out.shape[0]
    conv_width = width // 2
    nj = conv_width // MIX_COLS
    heads_per_step = MIX_COLS // GROUP_DIM
    assert m % tm == 0 and seq % tm == 0 and tm % CHUNK == 0

    def col_block(offset):
        return pl.BlockSpec((d, MIX_COLS), lambda i, j: (0, offset + j))

    vmem = (2 * 2 * tm * d * 4 + tm * d * 2 + tm * MIX_COLS * 2
            + 2 * 5 * d * MIX_COLS * w_in.dtype.itemsize
            + 2 * 2 * MIX_COLS * d * w_out.dtype.itemsize)
    return pl.pallas_call(
        functools.partial(_mix_kernel, tiles_per_seq=seq // tm),
        grid=(m // tm, nj),
        in_specs=[
            pl.BlockSpec((tm, d), lambda i, j: (i, 0)),
            pl.BlockSpec((1, d), lambda i, j: (0, 0)),
            col_block(0), col_block(nj), col_block(2 * nj),
            col_block(3 * nj), col_block(4 * nj),
            pl.BlockSpec((CONV_K, MIX_COLS), lambda i, j: (0, j)),
            pl.BlockSpec((1, MIX_COLS), lambda i, j: (0, j)),
            pl.BlockSpec((1, MIX_COLS), lambda i, j: (0, j)),
            pl.BlockSpec((heads_per_step, CHUNK, CHUNK), lambda i, j: (j, 0, 0)),
            pl.BlockSpec((None, CHUNK, heads_per_step), lambda i, j: (j, 0, 0)),
            pl.BlockSpec((MIX_COLS, d), lambda i, j: (j, 0)),
            pl.BlockSpec((MIX_COLS, d), lambda i, j: (nj + j, 0)),
        ],
        out_specs=pl.BlockSpec((tm, d), lambda i, j: (i, 0)),
        out_shape=jax.ShapeDtypeStruct((m, d), F32),
        scratch_shapes=[pltpu.VMEM((tm, d), BF16), pltpu.VMEM((tm, MIX_COLS), BF16),
                        pltpu.VMEM((nj, SUBLANES, MIX_COLS), F32)],
        compiler_params=pltpu.CompilerParams(
            dimension_semantics=("arbitrary", "arbitrary"),
            vmem_limit_bytes=_vmem_limit(vmem)),
        name="mix",
    )(h, g, w_in, w_in, w_in, w_in, w_in, conv_w, conv_b, g_v, w_s, b_s_t, w_out, w_out)


def _kv_kernel(mem_ref, g_ref, wk_ref, wv_ref, k_ref, v_ref, mn_ref):
    @pl.when(pl.program_id(0) == 0)
    def _():
        mn_ref[...] = _rmsnorm(mem_ref[...], g_ref[...]).astype(BF16)

    mn = mn_ref[...]
    k_ref[...] = _dot(mn, wk_ref[...].astype(BF16)).astype(BF16)
    v_ref[...] = _dot(mn, wv_ref[...].astype(BF16)).astype(BF16)


def _kv(mem, g, w_k, w_v, *, tn):
    m, d = mem.shape
    vmem = (2 * m * d * 4 + m * d * 2 + 2 * 2 * d * tn * w_k.dtype.itemsize
            + 2 * 2 * m * tn * 2)
    return pl.pallas_call(
        _kv_kernel,
        grid=(d // tn,),
        in_specs=[
            pl.BlockSpec((m, d), lambda n: (0, 0)),
            pl.BlockSpec((1, d), lambda n: (0, 0)),
            pl.BlockSpec((d, tn), lambda n: (0, n)),
            pl.BlockSpec((d, tn), lambda n: (0, n)),
        ],
        out_specs=[pl.BlockSpec((m, tn), lambda n: (0, n))] * 2,
        out_shape=[jax.ShapeDtypeStruct((m, d), BF16)] * 2,
        scratch_shapes=[pltpu.VMEM((m, d), BF16)],
        compiler_params=pltpu.CompilerParams(
            dimension_semantics=("arbitrary",), vmem_limit_bytes=_vmem_limit(vmem)),
        name="kv",
    )(mem, g, w_k, w_v)


def _xattn_kernel(h_ref, g_ref, wq_ref, k_ref, v_ref, wo_ref, o_ref, hn_ref, acc_ref):
    hd = pl.program_id(1)

    @pl.when(hd == 0)
    def _():
        hn_ref[...] = _rmsnorm(h_ref[...], g_ref[...]).astype(BF16)
        acc_ref[...] = jnp.zeros_like(acc_ref)

    q = _dot(hn_ref[...], wq_ref[...].astype(BF16))
    scores = lax.dot_general(q.astype(BF16), k_ref[...], (((1,), (1,)), ((), ())),
                             preferred_element_type=F32)
    scores = scores * (q.shape[-1] ** -0.5)
    p = jnp.exp(scores - jnp.max(scores, axis=-1, keepdims=True))
    p = p / jnp.sum(p, axis=-1, keepdims=True)
    o = _dot(p.astype(BF16), v_ref[...])
    acc_ref[...] += _dot(o.astype(BF16), wo_ref[...].astype(BF16))

    @pl.when(hd == pl.num_programs(1) - 1)
    def _():
        o_ref[...] = h_ref[...] + acc_ref[...]


def _xattn(h, g, w_q, k, v, w_o, *, seq, n_mem, tm):
    m, d = h.shape
    hdim = d // XA_HEADS
    tiles_per_seq = seq // tm
    assert m % tm == 0 and seq % tm == 0
    vmem = (2 * 2 * tm * d * 4 + tm * d * (2 + 4)
            + 2 * 2 * d * hdim * w_q.dtype.itemsize + 2 * 2 * n_mem * hdim * 2)
    return pl.pallas_call(
        _xattn_kernel,
        grid=(m // tm, XA_HEADS),
        in_specs=[
            pl.BlockSpec((tm, d), lambda i, hd: (i, 0)),
            pl.BlockSpec((1, d), lambda i, hd: (0, 0)),
            pl.BlockSpec((d, hdim), lambda i, hd: (0, hd)),
            pl.BlockSpec((n_mem, hdim), lambda i, hd: (i // tiles_per_seq, hd)),
            pl.BlockSpec((n_mem, hdim), lambda i, hd: (i // tiles_per_seq, hd)),
            pl.BlockSpec((hdim, d), lambda i, hd: (hd, 0)),
        ],
        out_specs=pl.BlockSpec((tm, d), lambda i, hd: (i, 0)),
        out_shape=jax.ShapeDtypeStruct((m, d), F32),
        scratch_shapes=[pltpu.VMEM((tm, d), BF16), pltpu.VMEM((tm, d), F32)],
        compiler_params=pltpu.CompilerParams(
            dimension_semantics=("arbitrary", "arbitrary"),
            vmem_limit_bytes=_vmem_limit(vmem)),
        name="xattn",
    )(h, g, w_q, k, v, w_o)


def kernel(x, mem, g_ffn1, w_ffn1_in, w_ffn1_out, g_mix, w_mix_in, conv_w, conv_b, g_gm_v,
           w_spatial, b_spatial, w_mix_out, g_xattn, g_mem, w_xq, w_xk, w_xv, w_xo, g_ffn2,
           w_ffn2_in, w_ffn2_out, g_final):
    b, s, d = x.shape
    n_mem = mem.shape[1]
    depth = g_ffn1.shape[0]
    assert depth >= 1
    heads_per_step = MIX_COLS // GROUP_DIM
    ffn_tiles = dict(tm=1024, tf=256)
    tm = 512

    def row(v):
        return v.reshape(1, -1)

    h = x.reshape(b * s, d)
    mem2 = mem.reshape(b * n_mem, d)
    g_fin = row(g_final)
    for l in range(depth):
        last = l == depth - 1
        h = _ffn(h, row(g_ffn1[l]), w_ffn1_in[l], w_ffn1_out[l], g_fin,
                 final_norm=False, **ffn_tiles)
        b_s_t = b_spatial[l].reshape(-1, heads_per_step, CHUNK).transpose(0, 2, 1)
        h = _mix(h, row(g_mix[l]), w_mix_in[l], conv_w[l], row(conv_b[l]), row(g_gm_v[l]),
                 w_spatial[l], b_s_t, w_mix_out[l], seq=s, tm=tm)
        k, v = _kv(mem2, row(g_mem[l]), w_xk[l], w_xv[l], tn=512)
        h = _xattn(h, row(g_xattn[l]), w_xq[l], k, v, w_xo[l], seq=s, n_mem=n_mem, tm=tm)
        h = _ffn(h, row(g_ffn2[l]), w_ffn2_in[l], w_ffn2_out[l], g_fin,
                 final_norm=last, **ffn_tiles)
    return h.reshape(b, s, d)
```

```python
import functools

import jax
import jax.numpy as jnp
from jax import lax
from jax.experimental import pallas as pl
from jax.experimental.pallas import tpu as pltpu

GROUP_DIM = 128
CHUNK = 128
CONV_K = 3
XA_HEADS = 4
EPS = 1e-6

V7X_VMEM_BYTES = 64 * 1024 * 1024
SUBLANES = 8
MIX_COLS = 256

BF16 = jnp.bfloat16
F32 = jnp.float32


def _rmsnorm(x, g):
    y = x * lax.rsqrt(jnp.mean(x * x, axis=-1, keepdims=True) + EPS)
    return y * g


_dot = functools.partial(jnp.dot, preferred_element_type=F32)


def _vmem_limit(nbytes):
    return min(int(nbytes * 1.25) + (8 << 20), V7X_VMEM_BYTES - (4 << 20))


def _ffn_kernel(x_ref, g_ref, wg_ref, wu_ref, wo_ref, gf_ref, o_ref, xn_ref, *, final_norm):
    k = pl.program_id(1)

    @pl.when(k == 0)
    def _():
        x = x_ref[...]
        xn_ref[...] = _rmsnorm(x, g_ref[...]).astype(BF16)
        o_ref[...] = x

    xn = xn_ref[...]
    gate = _dot(xn, wg_ref[...].astype(BF16))
    up = _dot(xn, wu_ref[...].astype(BF16))
    hidden = (0.5 * gate * jax.nn.sigmoid(gate) * up).astype(BF16)
    o_ref[...] += _dot(hidden, wo_ref[...].astype(BF16))

    if final_norm:
        @pl.when(k == pl.num_programs(1) - 1)
        def _():
            o_ref[...] = _rmsnorm(o_ref[...], gf_ref[...])


def _ffn(x, g, w_in, w_out, g_final, *, final_norm, tm, tf):
    m, d = x.shape
    d_ff = w_out.shape[0]
    nk = d_ff // tf
    assert m % tm == 0 and d_ff % tf == 0
    vmem = (2 * 2 * tm * d * 4
            + tm * d * 2
            + 2 * 3 * d * tf * w_in.dtype.itemsize)
    return pl.pallas_call(
        functools.partial(_ffn_kernel, final_norm=final_norm),
        grid=(m // tm, nk),
        in_specs=[
            pl.BlockSpec((tm, d), lambda i, k: (i, 0)),
            pl.BlockSpec((1, d), lambda i, k: (0, 0)),
            pl.BlockSpec((d, tf), lambda i, k: (0, k)),
            pl.BlockSpec((d, tf), lambda i, k: (0, nk + k)),
            pl.BlockSpec((tf, d), lambda i, k: (k, 0)),
            pl.BlockSpec((1, d), lambda i, k: (0, 0)),
        ],
        out_specs=pl.BlockSpec((tm, d), lambda i, k: (i, 0)),
        out_shape=jax.ShapeDtypeStruct((m, d), F32),
        scratch_shapes=[pltpu.VMEM((tm, d), BF16)],
        compiler_params=pltpu.CompilerParams(
            dimension_semantics=("arbitrary", "arbitrary"),
            vmem_limit_bytes=_vmem_limit(vmem)),
        name="ffn_final" if final_norm else "ffn",
    )(x, g, w_in, w_in, w_out, g_final)


def _mix_kernel(h_ref, g_ref, wb_ref, wc_ref, wh_ref, wu_ref, wv_ref, cw_ref, cb_ref,
                gv_ref, ws_ref, bs_ref, woc_ref, wog_ref, o_ref, hn_ref, y_ref, carry_ref,
                *, nj, n_steps, tiles_per_seq):
    s = pl.program_id(0)
    j = lax.rem(s, nj)
    cur = lax.rem(s, 2)
    prev = 1 - cur
    tm = h_ref.shape[0]

    @pl.when(s == 0)
    def _():
        y_ref[prev] = jnp.zeros(y_ref.shape[1:], BF16)
        o_ref[...] = jnp.zeros_like(o_ref)

    @pl.when((j == 0) & (s < n_steps))
    def _():
        hn_ref[...] = _rmsnorm(h_ref[...], g_ref[...]).astype(BF16)

    @pl.when(j == 1)
    def _():
        o_ref[...] = h_ref[...]

    @pl.when(lax.rem(s // nj, tiles_per_seq) == 0)
    def _():
        carry_ref[j] = jnp.zeros((SUBLANES, MIX_COLS), F32)

    w_in = jnp.concatenate([r[...].astype(BF16) for r in (wv_ref, wu_ref, wc_ref, wh_ref, wb_ref)],
                           axis=1)
    proj = _dot(hn_ref[...], w_in)
    v, u, gate_c, h_c, gate_b = (proj[:, n * MIX_COLS:(n + 1) * MIX_COLS] for n in range(5))

    y_prev = y_ref[prev]
    o_ref[...] += _dot(y_prev[:, :MIX_COLS], woc_ref[...].astype(BF16))

    gv = gv_ref[...]
    bs = bs_ref[...]
    tri = (lax.broadcasted_iota(jnp.int32, (CHUNK, CHUNK), 0)
           >= lax.broadcasted_iota(jnp.int32, (CHUNK, CHUNK), 1))
    sgs = []
    for hh in range(MIX_COLS // GROUP_DIM):
        lanes = slice(hh * GROUP_DIM, (hh + 1) * GROUP_DIM)
        vh = _rmsnorm(v[:, lanes], gv[:, lanes]).astype(BF16)
        w = jnp.where(tri, ws_ref[hh], 0.0).astype(BF16)
        sgs.append([_dot(w, vh[c * CHUNK:(c + 1) * CHUNK]) for c in range(tm // CHUNK)])

    o_ref[...] += _dot(y_prev[:, MIX_COLS:], wog_ref[...].astype(BF16))

    for hh, per_chunk in enumerate(sgs):
        lanes = slice(hh * GROUP_DIM, (hh + 1) * GROUP_DIM)
        for c, sg in enumerate(per_chunk):
            rows = slice(c * CHUNK, (c + 1) * CHUNK)
            y_ref[cur, rows, MIX_COLS + hh * GROUP_DIM:MIX_COLS + (hh + 1) * GROUP_DIM] = (
                u[rows, lanes] * (sg + bs[:, hh:hh + 1])).astype(BF16)

    z = gate_c * h_c
    tail = carry_ref[j]
    carry_ref[j] = z[tm - SUBLANES:, :]
    cw = cw_ref[...]
    cb = cb_ref[...]
    row = lax.broadcasted_iota(jnp.int32, (SUBLANES, MIX_COLS), 0)
    z1 = pltpu.roll(z, 1, 0)
    z2 = pltpu.roll(z, 2, 0)
    head1 = jnp.where(row < 1, pltpu.roll(tail, 1, 0), z1[:SUBLANES])
    head2 = jnp.where(row < 2, pltpu.roll(tail, 2, 0), z2[:SUBLANES])
    z1 = jnp.concatenate([head1, z1[SUBLANES:]], axis=0)
    z2 = jnp.concatenate([head2, z2[SUBLANES:]], axis=0)
    y_ref[cur, :, :MIX_COLS] = (
        gate_b * (cb + cw[2:3] * z + cw[1:2] * z1 + cw[0:1] * z2)).astype(BF16)


def _mix(h, g, w_in, conv_w, conv_b, g_v, w_s, b_s_t, w_out, *, seq, tm):
    m, d = h.shape
    width = w_out.shape[0]
    conv_width = width // 2
    nj = conv_width // MIX_COLS
    heads_per_step = MIX_COLS // GROUP_DIM
    n_tiles = m // tm
    n_steps = n_tiles * nj
    assert m % tm == 0 and seq % tm == 0 and tm % CHUNK == 0 and nj >= 2

    def cur_tile(s):
        return jnp.minimum(s // nj, n_tiles - 1)

    def prev_tile(s):
        return jnp.maximum(s - 1, 0) // nj

    def prev_j(s):
        return jnp.maximum(s - 1, 0) % nj

    def col_block(offset):
        return pl.BlockSpec((d, MIX_COLS), lambda s: (0, offset + s % nj))

    vmem = (2 * 2 * tm * d * 4 + tm * d * 2 + 2 * tm * 2 * MIX_COLS * 2
            + 2 * 5 * d * MIX_COLS * w_in.dtype.itemsize
            + 2 * 2 * MIX_COLS * d * w_out.dtype.itemsize)
    return pl.pallas_call(
        functools.partial(_mix_kernel, nj=nj, n_steps=n_steps, tiles_per_seq=seq // tm),
        grid=(n_steps + 1,),
        in_specs=[
            pl.BlockSpec((tm, d), lambda s: (cur_tile(s), 0)),
            pl.BlockSpec((1, d), lambda s: (0, 0)),
            col_block(0), col_block(nj), col_block(2 * nj),
            col_block(3 * nj), col_block(4 * nj),
            pl.BlockSpec((CONV_K, MIX_COLS), lambda s: (0, s % nj)),
            pl.BlockSpec((1, MIX_COLS), lambda s: (0, s % nj)),
            pl.BlockSpec((1, MIX_COLS), lambda s: (0, s % nj)),
            pl.BlockSpec((heads_per_step, CHUNK, CHUNK), lambda s: (s % nj, 0, 0)),
            pl.BlockSpec((None, CHUNK, heads_per_step), lambda s: (s % nj, 0, 0)),
            pl.BlockSpec((MIX_COLS, d), lambda s: (prev_j(s), 0)),
            pl.BlockSpec((MIX_COLS, d), lambda s: (nj + prev_j(s), 0)),
        ],
        out_specs=pl.BlockSpec((tm, d), lambda s: (prev_tile(s), 0)),
        out_shape=jax.ShapeDtypeStruct((m, d), F32),
        scratch_shapes=[pltpu.VMEM((tm, d), BF16), pltpu.VMEM((2, tm, 2 * MIX_COLS), BF16),
                        pltpu.VMEM((nj, SUBLANES, MIX_COLS), F32)],
        compiler_params=pltpu.CompilerParams(
            dimension_semantics=("arbitrary",), vmem_limit_bytes=_vmem_limit(vmem)),
        name="mix",
    )(h, g, w_in, w_in, w_in, w_in, w_in, conv_w, conv_b, g_v, w_s, b_s_t, w_out, w_out)


def _kv_kernel(mem_ref, g_ref, wk_ref, wv_ref, k_ref, v_ref, mn_ref):
    @pl.when(pl.program_id(0) == 0)
    def _():
        mn_ref[...] = _rmsnorm(mem_ref[...], g_ref[...]).astype(BF16)

    mn = mn_ref[...]
    k_ref[...] = _dot(mn, wk_ref[...].astype(BF16)).astype(BF16)
    v_ref[...] = _dot(mn, wv_ref[...].astype(BF16)).astype(BF16)


def _kv(mem, g, w_k, w_v, *, tn):
    m, d = mem.shape
    vmem = (2 * m * d * 4 + m * d * 2 + 2 * 2 * d * tn * w_k.dtype.itemsize
            + 2 * 2 * m * tn * 2)
    return pl.pallas_call(
        _kv_kernel,
        grid=(d // tn,),
        in_specs=[
            pl.BlockSpec((m, d), lambda n: (0, 0)),
            pl.BlockSpec((1, d), lambda n: (0, 0)),
            pl.BlockSpec((d, tn), lambda n: (0, n)),
            pl.BlockSpec((d, tn), lambda n: (0, n)),
        ],
        out_specs=[pl.BlockSpec((m, tn), lambda n: (0, n))] * 2,
        out_shape=[jax.ShapeDtypeStruct((m, d), BF16)] * 2,
        scratch_shapes=[pltpu.VMEM((m, d), BF16)],
        compiler_params=pltpu.CompilerParams(
            dimension_semantics=("arbitrary",), vmem_limit_bytes=_vmem_limit(vmem)),
        name="kv",
    )(mem, g, w_k, w_v)


def _xattn_kernel(h_ref, g_ref, wq_ref, k_ref, v_ref, wo_ref, o_ref, hn_ref, q_ref, *, n_steps):
    s = pl.program_id(0)
    cur = lax.rem(s, 2)
    prev = 1 - cur

    @pl.when(s == 0)
    def _():
        q_ref[prev] = jnp.zeros(q_ref.shape[1:], BF16)
        o_ref[...] = jnp.zeros_like(o_ref)

    @pl.when((lax.rem(s, XA_HEADS) == 0) & (s < n_steps))
    def _():
        hn_ref[...] = _rmsnorm(h_ref[...], g_ref[...]).astype(BF16)

    @pl.when(lax.rem(s, XA_HEADS) == 1)
    def _():
        o_ref[...] = h_ref[...]

    q_prev = q_ref[prev]
    scores = lax.dot_general(q_prev, k_ref[...], (((1,), (1,)), ((), ())),
                             preferred_element_type=F32)
    q_ref[cur] = _dot(hn_ref[...], wq_ref[...].astype(BF16)).astype(BF16)
    scores = scores * (q_prev.shape[-1] ** -0.5)
    p = jnp.exp(scores - jnp.max(scores, axis=-1, keepdims=True))
    p = p / jnp.sum(p, axis=-1, keepdims=True)
    o = _dot(p.astype(BF16), v_ref[...])
    o_ref[...] += _dot(o.astype(BF16), wo_ref[...].astype(BF16))


def _xattn(h, g, w_q, k, v, w_o, *, seq, n_mem, tm):
    m, d = h.shape
    hdim = d // XA_HEADS
    tiles_per_seq = seq // tm
    n_tiles = m // tm
    n_steps = n_tiles * XA_HEADS
    assert m % tm == 0 and seq % tm == 0

    def cur_tile(s):
        return jnp.minimum(s // XA_HEADS, n_tiles - 1)

    def prev_tile(s):
        return jnp.maximum(s - 1, 0) // XA_HEADS

    def prev_head(s):
        return jnp.maximum(s - 1, 0) % XA_HEADS

    vmem = (2 * 2 * tm * d * 4 + tm * d * 2 + 2 * tm * hdim * 2
            + 2 * 2 * d * hdim * w_q.dtype.itemsize + 2 * 2 * n_mem * hdim * 2)
    return pl.pallas_call(
        functools.partial(_xattn_kernel, n_steps=n_steps),
        grid=(n_steps + 1,),
        in_specs=[
            pl.BlockSpec((tm, d), lambda s: (cur_tile(s), 0)),
            pl.BlockSpec((1, d), lambda s: (0, 0)),
            pl.BlockSpec((d, hdim), lambda s: (0, s % XA_HEADS)),
            pl.BlockSpec((n_mem, hdim), lambda s: (prev_tile(s) // tiles_per_seq, prev_head(s))),
            pl.BlockSpec((n_mem, hdim), lambda s: (prev_tile(s) // tiles_per_seq, prev_head(s))),
            pl.BlockSpec((hdim, d), lambda s: (prev_head(s), 0)),
        ],
        out_specs=pl.BlockSpec((tm, d), lambda s: (prev_tile(s), 0)),
        out_shape=jax.ShapeDtypeStruct((m, d), F32),
        scratch_shapes=[pltpu.VMEM((tm, d), BF16), pltpu.VMEM((2, tm, hdim), BF16)],
        compiler_params=pltpu.CompilerParams(
            dimension_semantics=("arbitrary",), vmem_limit_bytes=_vmem_limit(vmem)),
        name="xattn",
    )(h, g, w_q, k, v, w_o)


def kernel(x, mem, g_ffn1, w_ffn1_in, w_ffn1_out, g_mix, w_mix_in, conv_w, conv_b, g_gm_v,
           w_spatial, b_spatial, w_mix_out, g_xattn, g_mem, w_xq, w_xk, w_xv, w_xo, g_ffn2,
           w_ffn2_in, w_ffn2_out, g_final):
    b, s, d = x.shape
    n_mem = mem.shape[1]
    depth = g_ffn1.shape[0]
    assert depth >= 1
    heads_per_step = MIX_COLS // GROUP_DIM
    ffn_tiles = dict(tm=1024, tf=256)
    tm = 512

    def row(v):
        return v.reshape(1, -1)

    h = x.reshape(b * s, d)
    mem2 = mem.reshape(b * n_mem, d)
    g_fin = row(g_final)
    for l in range(depth):
        last = l == depth - 1
        h = _ffn(h, row(g_ffn1[l]), w_ffn1_in[l], w_ffn1_out[l], g_fin,
                 final_norm=False, **ffn_tiles)
        b_s_t = b_spatial[l].reshape(-1, heads_per_step, CHUNK).transpose(0, 2, 1)
        h = _mix(h, row(g_mix[l]), w_mix_in[l], conv_w[l], row(conv_b[l]), row(g_gm_v[l]),
                 w_spatial[l], b_s_t, w_mix_out[l], seq=s, tm=tm)
        k, v = _kv(mem2, row(g_mem[l]), w_xk[l], w_xv[l], tn=512)
        h = _xattn(h, row(g_xattn[l]), w_xq[l], k, v, w_xo[l], seq=s, n_mem=n_mem, tm=tm)
        h = _ffn(h, row(g_ffn2[l]), w_ffn2_in[l], w_ffn2_out[l], g_fin,
                 final_norm=last, **ffn_tiles)
    return h.reshape(b, s, d)
```

```python
import functools

import jax
import jax.numpy as jnp
from jax import lax
from jax.experimental import pallas as pl
from jax.experimental.pallas import tpu as pltpu

GROUP_DIM = 128
CHUNK = 128
CONV_K = 3
XA_HEADS = 4
EPS = 1e-6

V7X_VMEM_BYTES = 64 * 1024 * 1024
SUBLANES = 8
MIX_COLS = 256

BF16 = jnp.bfloat16
F32 = jnp.float32


def _rmsnorm(x, g):
    y = x * lax.rsqrt(jnp.mean(x * x, axis=-1, keepdims=True) + EPS)
    return y * g


_dot = functools.partial(jnp.dot, preferred_element_type=F32)


def _vmem_limit(nbytes):
    return min(int(nbytes * 1.25) + (8 << 20), V7X_VMEM_BYTES - (4 << 20))


def _ffn_kernel(x_ref, g_ref, wg_ref, wu_ref, wo_ref, gf_ref, o_ref, xn_ref, *, final_norm):
    k = pl.program_id(1)

    @pl.when(k == 0)
    def _():
        x = x_ref[...]
        xn_ref[...] = _rmsnorm(x, g_ref[...]).astype(BF16)
        o_ref[...] = x

    xn = xn_ref[...]
    gate = _dot(xn, wg_ref[...].astype(BF16))
    up = _dot(xn, wu_ref[...].astype(BF16))
    hidden = (0.5 * gate * jax.nn.sigmoid(gate) * up).astype(BF16)
    o_ref[...] += _dot(hidden, wo_ref[...].astype(BF16))

    if final_norm:
        @pl.when(k == pl.num_programs(1) - 1)
        def _():
            o_ref[...] = _rmsnorm(o_ref[...], gf_ref[...])


def _ffn(x, g, w_in, w_out, g_final, *, final_norm, tm, tf):
    m, d = x.shape
    d_ff = w_out.shape[0]
    nk = d_ff // tf
    assert m % tm == 0 and d_ff % tf == 0
    vmem = (2 * 2 * tm * d * 4
            + tm * d * 2
            + 2 * 3 * d * tf * w_in.dtype.itemsize)
    return pl.pallas_call(
        functools.partial(_ffn_kernel, final_norm=final_norm),
        grid=(m // tm, nk),
        in_specs=[
            pl.BlockSpec((tm, d), lambda i, k: (i, 0)),
            pl.BlockSpec((1, d), lambda i, k: (0, 0)),
            pl.BlockSpec((d, tf), lambda i, k: (0, k)),
            pl.BlockSpec((d, tf), lambda i, k: (0, nk + k)),
            pl.BlockSpec((tf, d), lambda i, k: (k, 0)),
            pl.BlockSpec((1, d), lambda i, k: (0, 0)),
        ],
        out_specs=pl.BlockSpec((tm, d), lambda i, k: (i, 0)),
        out_shape=jax.ShapeDtypeStruct((m, d), F32),
        scratch_shapes=[pltpu.VMEM((tm, d), BF16)],
        compiler_params=pltpu.CompilerParams(
            dimension_semantics=("arbitrary", "arbitrary"),
            vmem_limit_bytes=_vmem_limit(vmem)),
        name="ffn_final" if final_norm else "ffn",
    )(x, g, w_in, w_in, w_out, g_final)


def _mix_kernel(h_ref, g_ref, wb_ref, wc_ref, wh_ref, wu_ref, wv_ref, cw_ref, cb_ref,
                gv_ref, ws_ref, bs_ref, woc_ref, wog_ref, o_ref, hn_ref, y_ref, carry_ref,
                *, nj, n_steps, tiles_per_seq):
    s = pl.program_id(0)
    j = lax.rem(s, nj)
    cur = lax.rem(s, 2)
    prev = 1 - cur
    tm = h_ref.shape[0]

    @pl.when(s == 0)
    def _():
        y_ref[prev] = jnp.zeros(y_ref.shape[1:], BF16)
        o_ref[...] = jnp.zeros_like(o_ref)

    @pl.when((j == 0) & (s < n_steps))
    def _():
        hn_ref[...] = _rmsnorm(h_ref[...], g_ref[...]).astype(BF16)

    @pl.when(j == 1)
    def _():
        o_ref[...] = h_ref[...]

    @pl.when(lax.rem(s // nj, tiles_per_seq) == 0)
    def _():
        carry_ref[j] = jnp.zeros((SUBLANES, MIX_COLS), F32)

    w_in = jnp.concatenate([r[...].astype(BF16) for r in (wv_ref, wu_ref, wc_ref, wh_ref, wb_ref)],
                           axis=1)
    proj = _dot(hn_ref[...], w_in)
    v, u, gate_c, h_c, gate_b = (proj[:, n * MIX_COLS:(n + 1) * MIX_COLS] for n in range(5))

    y_prev = y_ref[prev]
    o_ref[...] += _dot(y_prev[:, :MIX_COLS], woc_ref[...].astype(BF16))

    gv = gv_ref[...]
    bs = bs_ref[...]
    tri = (lax.broadcasted_iota(jnp.int32, (CHUNK, CHUNK), 0)
           >= lax.broadcasted_iota(jnp.int32, (CHUNK, CHUNK), 1))
    sgs = []
    for hh in range(MIX_COLS // GROUP_DIM):
        lanes = slice(hh * GROUP_DIM, (hh + 1) * GROUP_DIM)
        vh = _rmsnorm(v[:, lanes], gv[:, lanes]).astype(BF16)
        w = jnp.where(tri, ws_ref[hh], 0.0).astype(BF16)
        sgs.append([_dot(w, vh[c * CHUNK:(c + 1) * CHUNK]) for c in range(tm // CHUNK)])

    o_ref[...] += _dot(y_prev[:, MIX_COLS:], wog_ref[...].astype(BF16))

    for hh, per_chunk in enumerate(sgs):
        lanes = slice(hh * GROUP_DIM, (hh + 1) * GROUP_DIM)
        for c, sg in enumerate(per_chunk):
            rows = slice(c * CHUNK, (c + 1) * CHUNK)
            y_ref[cur, rows, MIX_COLS + hh * GROUP_DIM:MIX_COLS + (hh + 1) * GROUP_DIM] = (
                u[rows, lanes] * (sg + bs[:, hh:hh + 1])).astype(BF16)

    z = gate_c * h_c
    tail = carry_ref[j]
    carry_ref[j] = z[tm - SUBLANES:, :]
    cw = cw_ref[...]
    cb = cb_ref[...]
    row = lax.broadcasted_iota(jnp.int32, (SUBLANES, MIX_COLS), 0)
    z1 = pltpu.roll(z, 1, 0)
    z2 = pltpu.roll(z, 2, 0)
    head1 = jnp.where(row < 1, pltpu.roll(tail, 1, 0), z1[:SUBLANES])
    head2 = jnp.where(row < 2, pltpu.roll(tail, 2, 0), z2[:SUBLANES])
    z1 = jnp.concatenate([head1, z1[SUBLANES:]], axis=0)
    z2 = jnp.concatenate([head2, z2[SUBLANES:]], axis=0)
    y_ref[cur, :, :MIX_COLS] = (
        gate_b * (cb + cw[2:3] * z + cw[1:2] * z1 + cw[0:1] * z2)).astype(BF16)


def _mix(h, g, w_in, conv_w, conv_b, g_v, w_s, b_s_t, w_out, *, seq, tm):
    m, d = h.shape
    width = w_out.shape[0]
    conv_width = width // 2
    nj = conv_width // MIX_COLS
    heads_per_step = MIX_COLS // GROUP_DIM
    n_tiles = m // tm
    n_steps = n_tiles * nj
    assert m % tm == 0 and seq % tm == 0 and tm % CHUNK == 0 and nj >= 2

    def cur_tile(s):
        return jnp.minimum(s // nj, n_tiles - 1)

    def prev_tile(s):
        return jnp.maximum(s - 1, 0) // nj

    def prev_j(s):
        return jnp.maximum(s - 1, 0) % nj

    def col_block(offset):
        return pl.BlockSpec((d, MIX_COLS), lambda s: (0, offset + s % nj))

    vmem = (2 * 2 * tm * d * 4 + tm * d * 2 + 2 * tm * 2 * MIX_COLS * 2
            + 2 * 5 * d * MIX_COLS * w_in.dtype.itemsize
            + 2 * 2 * MIX_COLS * d * w_out.dtype.itemsize)
    return pl.pallas_call(
        functools.partial(_mix_kernel, nj=nj, n_steps=n_steps, tiles_per_seq=seq // tm),
        grid=(n_steps + 1,),
        in_specs=[
            pl.BlockSpec((tm, d), lambda s: (cur_tile(s), 0)),
            pl.BlockSpec((1, d), lambda s: (0, 0)),
            col_block(0), col_block(nj), col_block(2 * nj),
            col_block(3 * nj), col_block(4 * nj),
            pl.BlockSpec((CONV_K, MIX_COLS), lambda s: (0, s % nj)),
            pl.BlockSpec((1, MIX_COLS), lambda s: (0, s % nj)),
            pl.BlockSpec((1, MIX_COLS), lambda s: (0, s % nj)),
            pl.BlockSpec((heads_per_step, CHUNK, CHUNK), lambda s: (s % nj, 0, 0)),
            pl.BlockSpec((None, CHUNK, heads_per_step), lambda s: (s % nj, 0, 0)),
            pl.BlockSpec((MIX_COLS, d), lambda s: (prev_j(s), 0)),
            pl.BlockSpec((MIX_COLS, d), lambda s: (nj + prev_j(s), 0)),
        ],
        out_specs=pl.BlockSpec((tm, d), lambda s: (prev_tile(s), 0)),
        out_shape=jax.ShapeDtypeStruct((m, d), F32),
        scratch_shapes=[pltpu.VMEM((tm, d), BF16), pltpu.VMEM((2, tm, 2 * MIX_COLS), BF16),
                        pltpu.VMEM((nj, SUBLANES, MIX_COLS), F32)],
        compiler_params=pltpu.CompilerParams(
            dimension_semantics=("arbitrary",), vmem_limit_bytes=_vmem_limit(vmem)),
        name="mix",
    )(h, g, w_in, w_in, w_in, w_in, w_in, conv_w, conv_b, g_v, w_s, b_s_t, w_out, w_out)


def _kv_kernel(mem_ref, g_ref, wk_ref, wv_ref, k_ref, v_ref, mn_ref):
    @pl.when(pl.program_id(0) == 0)
    def _():
        mn_ref[...] = _rmsnorm(mem_ref[...], g_ref[...]).astype(BF16)

    mn = mn_ref[...]
    k_ref[...] = _dot(mn, wk_ref[...].astype(BF16)).astype(BF16)
    v_ref[...] = _dot(mn, wv_ref[...].astype(BF16)).astype(BF16)


def _kv(mem, g, w_k, w_v, *, tn):
    m, d = mem.shape
    vmem = (2 * m * d * 4 + m * d * 2 + 2 * 2 * d * tn * w_k.dtype.itemsize
            + 2 * 2 * m * tn * 2)
    return pl.pallas_call(
        _kv_kernel,
        grid=(d // tn,),
        in_specs=[
            pl.BlockSpec((m, d), lambda n: (0, 0)),
            pl.BlockSpec((1, d), lambda n: (0, 0)),
            pl.BlockSpec((d, tn), lambda n: (0, n)),
            pl.BlockSpec((d, tn), lambda n: (0, n)),
        ],
        out_specs=[pl.BlockSpec((m, tn), lambda n: (0, n))] * 2,
        out_shape=[jax.ShapeDtypeStruct((m, d), BF16)] * 2,
        scratch_shapes=[pltpu.VMEM((m, d), BF16)],
        compiler_params=pltpu.CompilerParams(
            dimension_semantics=("arbitrary",), vmem_limit_bytes=_vmem_limit(vmem)),
        name="kv",
    )(mem, g, w_k, w_v)


def _xattn_kernel(h_ref, g_ref, wq_ref, k_ref, v_ref, wo_ref, o_ref, hn_ref, q_ref, *, n_steps):
    s = pl.program_id(0)
    cur = lax.rem(s, 2)
    prev = 1 - cur

    @pl.when(s == 0)
    def _():
        q_ref[prev] = jnp.zeros(q_ref.shape[1:], BF16)
        o_ref[...] = jnp.zeros_like(o_ref)

    @pl.when((lax.rem(s, XA_HEADS) == 0) & (s < n_steps))
    def _():
        hn_ref[...] = _rmsnorm(h_ref[...], g_ref[...]).astype(BF16)

    @pl.when(lax.rem(s, XA_HEADS) == 1)
    def _():
        o_ref[...] = h_ref[...]

    q_prev = q_ref[prev]
    scores = lax.dot_general(q_prev, k_ref[...], (((1,), (1,)), ((), ())),
                             preferred_element_type=F32)
    q_ref[cur] = _dot(hn_ref[...], wq_ref[...].astype(BF16)).astype(BF16)
    scores = scores * (q_prev.shape[-1] ** -0.5)
    p = jnp.exp(scores - jnp.max(scores, axis=-1, keepdims=True))
    p = p / jnp.sum(p, axis=-1, keepdims=True)
    o = _dot(p.astype(BF16), v_ref[...])
    o_ref[...] += _dot(o.astype(BF16), wo_ref[...].astype(BF16))


def _xattn(h, g, w_q, k, v, w_o, *, seq, n_mem, tm):
    m, d = h.shape
    hdim = d // XA_HEADS
    tiles_per_seq = seq // tm
    n_tiles = m // tm
    n_steps = n_tiles * XA_HEADS
    assert m % tm == 0 and seq % tm == 0

    def cur_tile(s):
        return jnp.minimum(s // XA_HEADS, n_tiles - 1)

    def prev_tile(s):
        return jnp.maximum(s - 1, 0) // XA_HEADS

    def prev_head(s):
        return jnp.maximum(s - 1, 0) % XA_HEADS

    vmem = (2 * 2 * tm * d * 4 + tm * d * 2 + 2 * tm * hdim * 2
            + 2 * 2 * d * hdim * w_q.dtype.itemsize + 2 * 2 * n_mem * hdim * 2)
    return pl.pallas_call(
        functools.partial(_xattn_kernel, n_steps=n_steps),
        grid=(n_steps + 1,),
        in_specs=[
            pl.BlockSpec((tm, d), lambda s: (cur_tile(s), 0)),
            pl.BlockSpec((1, d), lambda s: (0, 0)),
            pl.BlockSpec((d, hdim), lambda s: (0, s % XA_HEADS)),
            pl.BlockSpec((n_mem, hdim), lambda s: (prev_tile(s) // tiles_per_seq, prev_head(s))),
            pl.BlockSpec((n_mem, hdim), lambda s: (prev_tile(s) // tiles_per_seq, prev_head(s))),
            pl.BlockSpec((hdim, d), lambda s: (prev_head(s), 0)),
        ],
        out_specs=pl.BlockSpec((tm, d), lambda s: (prev_tile(s), 0)),
        out_shape=jax.ShapeDtypeStruct((m, d), F32),
        scratch_shapes=[pltpu.VMEM((tm, d), BF16), pltpu.VMEM((2, tm, hdim), BF16)],
        compiler_params=pltpu.CompilerParams(
            dimension_semantics=("arbitrary",), vmem_limit_bytes=_vmem_limit(vmem)),
        name="xattn",
    )(h, g, w_q, k, v, w_o)


def kernel(x, mem, g_ffn1, w_ffn1_in, w_ffn1_out, g_mix, w_mix_in, conv_w, conv_b, g_gm_v,
           w_spatial, b_spatial, w_mix_out, g_xattn, g_mem, w_xq, w_xk, w_xv, w_xo, g_ffn2,
           w_ffn2_in, w_ffn2_out, g_final):
    b, s, d = x.shape
    n_mem = mem.shape[1]
    depth = g_ffn1.shape[0]
    assert depth >= 1
    heads_per_step = MIX_COLS // GROUP_DIM
    ffn_tiles = dict(tm=1024, tf=256)
    tm = 512

    def row(v):
        return v.reshape(1, -1)

    h = x.reshape(b * s, d)
    mem2 = mem.reshape(b * n_mem, d)
    g_fin = row(g_final)
    for l in range(depth):
        last = l == depth - 1
        h = _ffn(h, row(g_ffn1[l]), w_ffn1_in[l], w_ffn1_out[l], g_fin,
                 final_norm=False, **ffn_tiles)
        b_s_t = b_spatial[l].reshape(-1, heads_per_step, CHUNK).transpose(0, 2, 1)
        h = _mix(h, row(g_mix[l]), w_mix_in[l].astype(BF16), conv_w[l], row(conv_b[l]),
                 row(g_gm_v[l]), w_spatial[l], b_s_t, w_mix_out[l].astype(BF16), seq=s, tm=tm)
        k, v = _kv(mem2, row(g_mem[l]), w_xk[l], w_xv[l], tn=512)
        h = _xattn(h, row(g_xattn[l]), w_xq[l].astype(BF16), k, v, w_xo[l].astype(BF16),
                   seq=s, n_mem=n_mem, tm=tm)
        h = _ffn(h, row(g_ffn2[l]), w_ffn2_in[l], w_ffn2_out[l], g_fin,
                 final_norm=last, **ffn_tiles)
    return h.reshape(b, s, d)
```

```python
import functools

import jax
import jax.numpy as jnp
from jax import lax
from jax.experimental import pallas as pl
from jax.experimental.pallas import tpu as pltpu

GROUP_DIM = 128
CHUNK = 128
CONV_K = 3
XA_HEADS = 4
EPS = 1e-6

V7X_VMEM_BYTES = 64 * 1024 * 1024
SUBLANES = 8
MIX_COLS = 256

BF16 = jnp.bfloat16
F32 = jnp.float32


def _rmsnorm(x, g):
    y = x * lax.rsqrt(jnp.mean(x * x, axis=-1, keepdims=True) + EPS)
    return y * g


_dot = functools.partial(jnp.dot, preferred_element_type=F32)


def _vmem_limit(nbytes):
    return min(int(nbytes * 1.25) + (8 << 20), V7X_VMEM_BYTES - (4 << 20))


def _ffn_kernel(x_hbm, g_ref, wg_ref, wu_ref, wo_ref, gf_ref, o_hbm, buf_ref, xn_ref, in_sem,
                out_sem, *, final_norm):
    i = pl.program_id(0)
    k = pl.program_id(1)
    n_tiles = pl.num_programs(0)
    nk = pl.num_programs(1)
    tm = xn_ref.shape[0]
    slot = lax.rem(i, 2)
    other = 1 - slot

    def load(tile, dst):
        return pltpu.make_async_copy(x_hbm.at[pl.ds(tile * tm, tm)], buf_ref.at[dst], in_sem)

    def store(tile, src):
        return pltpu.make_async_copy(buf_ref.at[src], o_hbm.at[pl.ds(tile * tm, tm)], out_sem)

    @pl.when(k == 0)
    def _():
        @pl.when(i == 0)
        def _():
            load(0, 0).start()

        load(i, slot).wait()
        xn_ref[...] = _rmsnorm(buf_ref[slot], g_ref[...]).astype(BF16)

        @pl.when(i > 0)
        def _():
            store(i - 1, other).start()

    @pl.when(k == nk // 2)
    def _():
        @pl.when(i > 0)
        def _():
            store(i - 1, other).wait()

        @pl.when(i + 1 < n_tiles)
        def _():
            load(i + 1, other).start()

    xn = xn_ref[...]
    gate = _dot(xn, wg_ref[...].astype(BF16))
    up = _dot(xn, wu_ref[...].astype(BF16))
    hidden = (0.5 * gate * jax.nn.sigmoid(gate) * up).astype(BF16)
    buf_ref[slot] += _dot(hidden, wo_ref[...].astype(BF16))

    @pl.when(k == nk - 1)
    def _():
        if final_norm:
            buf_ref[slot] = _rmsnorm(buf_ref[slot], gf_ref[...])

        @pl.when(i == n_tiles - 1)
        def _():
            store(i, slot).start()
            store(i, slot).wait()


def _ffn(x, g, w_in, w_out, g_final, *, final_norm, tm, tf):
    m, d = x.shape
    d_ff = w_out.shape[0]
    nk = d_ff // tf
    assert m % tm == 0 and d_ff % tf == 0 and nk >= 2
    vmem = (2 * tm * d * 4
            + tm * d * 2
            + 2 * 3 * d * tf * w_in.dtype.itemsize)
    return pl.pallas_call(
        functools.partial(_ffn_kernel, final_norm=final_norm),
        grid=(m // tm, nk),
        in_specs=[
            pl.BlockSpec(memory_space=pl.ANY),
            pl.BlockSpec((1, d), lambda i, k: (0, 0)),
            pl.BlockSpec((d, tf), lambda i, k: (0, k)),
            pl.BlockSpec((d, tf), lambda i, k: (0, nk + k)),
            pl.BlockSpec((tf, d), lambda i, k: (k, 0)),
            pl.BlockSpec((1, d), lambda i, k: (0, 0)),
        ],
        out_specs=pl.BlockSpec(memory_space=pl.ANY),
        out_shape=jax.ShapeDtypeStruct((m, d), F32),
        scratch_shapes=[pltpu.VMEM((2, tm, d), F32), pltpu.VMEM((tm, d), BF16),
                        pltpu.SemaphoreType.DMA(()), pltpu.SemaphoreType.DMA(())],
        compiler_params=pltpu.CompilerParams(
            dimension_semantics=("arbitrary", "arbitrary"),
            vmem_limit_bytes=_vmem_limit(vmem)),
        name="ffn_final" if final_norm else "ffn",
    )(x, g, w_in, w_in, w_out, g_final)


def _mix_kernel(h_ref, g_ref, wb_ref, wc_ref, wh_ref, wu_ref, wv_ref, cw_ref, cb_ref,
                gv_ref, ws_ref, bs_ref, woc_ref, wog_ref, o_ref, hn_ref, y_ref, carry_ref,
                *, nj, n_steps, tiles_per_seq):
    s = pl.program_id(0)
    j = lax.rem(s, nj)
    cur = lax.rem(s, 2)
    prev = 1 - cur
    tm = h_ref.shape[0]

    @pl.when(s == 0)
    def _():
        y_ref[prev] = jnp.zeros(y_ref.shape[1:], BF16)
        o_ref[...] = jnp.zeros_like(o_ref)

    @pl.when((j == 0) & (s < n_steps))
    def _():
        hn_ref[...] = _rmsnorm(h_ref[...], g_ref[...]).astype(BF16)

    @pl.when(j == 1)
    def _():
        o_ref[...] = h_ref[...]

    @pl.when(lax.rem(s // nj, tiles_per_seq) == 0)
    def _():
        carry_ref[j] = jnp.zeros((SUBLANES, MIX_COLS), F32)

    w_in = jnp.concatenate([r[...].astype(BF16) for r in (wv_ref, wu_ref, wc_ref, wh_ref, wb_ref)],
                           axis=1)
    proj = _dot(hn_ref[...], w_in)
    v, u, gate_c, h_c, gate_b = (proj[:, n * MIX_COLS:(n + 1) * MIX_COLS] for n in range(5))

    y_prev = y_ref[prev]
    o_ref[...] += _dot(y_prev[:, :MIX_COLS], woc_ref[...].astype(BF16))

    gv = gv_ref[...]
    bs = bs_ref[...]
    tri = (lax.broadcasted_iota(jnp.int32, (CHUNK, CHUNK), 0)
           >= lax.broadcasted_iota(jnp.int32, (CHUNK, CHUNK), 1))
    sgs = []
    for hh in range(MIX_COLS // GROUP_DIM):
        lanes = slice(hh * GROUP_DIM, (hh + 1) * GROUP_DIM)
        vh = _rmsnorm(v[:, lanes], gv[:, lanes]).astype(BF16)
        w = jnp.where(tri, ws_ref[hh], 0.0).astype(BF16)
        sgs.append([_dot(w, vh[c * CHUNK:(c + 1) * CHUNK]) for c in range(tm // CHUNK)])

    o_ref[...] += _dot(y_prev[:, MIX_COLS:], wog_ref[...].astype(BF16))

    for hh, per_chunk in enumerate(sgs):
        lanes = slice(hh * GROUP_DIM, (hh + 1) * GROUP_DIM)
        for c, sg in enumerate(per_chunk):
            rows = slice(c * CHUNK, (c + 1) * CHUNK)
            y_ref[cur, rows, MIX_COLS + hh * GROUP_DIM:MIX_COLS + (hh + 1) * GROUP_DIM] = (
                u[rows, lanes] * (sg + bs[:, hh:hh + 1])).astype(BF16)

    z = gate_c * h_c
    tail = carry_ref[j]
    carry_ref[j] = z[tm - SUBLANES:, :]
    cw = cw_ref[...]
    cb = cb_ref[...]
    row = lax.broadcasted_iota(jnp.int32, (SUBLANES, MIX_COLS), 0)
    z1 = pltpu.roll(z, 1, 0)
    z2 = pltpu.roll(z, 2, 0)
    head1 = jnp.where(row < 1, pltpu.roll(tail, 1, 0), z1[:SUBLANES])
    head2 = jnp.where(row < 2, pltpu.roll(tail, 2, 0), z2[:SUBLANES])
    z1 = jnp.concatenate([head1, z1[SUBLANES:]], axis=0)
    z2 = jnp.concatenate([head2, z2[SUBLANES:]], axis=0)
    y_ref[cur, :, :MIX_COLS] = (
        gate_b * (cb + cw[2:3] * z + cw[1:2] * z1 + cw[0:1] * z2)).astype(BF16)


def _mix(h, g, w_in, conv_w, conv_b, g_v, w_s, b_s_t, w_out, *, seq, tm):
    m, d = h.shape
    width = w_out.shape[0]
    conv_width = width // 2
    nj = conv_width // MIX_COLS
    heads_per_step = MIX_COLS // GROUP_DIM
    n_tiles = m // tm
    n_steps = n_tiles * nj
    assert m % tm == 0 and seq % tm == 0 and tm % CHUNK == 0 and nj >= 2

    def cur_tile(s):
        return jnp.minimum(s // nj, n_tiles - 1)

    def prev_tile(s):
        return jnp.maximum(s - 1, 0) // nj

    def prev_j(s):
        return jnp.maximum(s - 1, 0) % nj

    def col_block(offset):
        return pl.BlockSpec((d, MIX_COLS), lambda s: (0, offset + s % nj))

    vmem = (2 * 2 * tm * d * 4 + tm * d * 2 + 2 * tm * 2 * MIX_COLS * 2
            + 2 * 5 * d * MIX_COLS * w_in.dtype.itemsize
            + 2 * 2 * MIX_COLS * d * w_out.dtype.itemsize)
    return pl.pallas_call(
        functools.partial(_mix_kernel, nj=nj, n_steps=n_steps, tiles_per_seq=seq // tm),
        grid=(n_steps + 1,),
        in_specs=[
            pl.BlockSpec((tm, d), lambda s: (cur_tile(s), 0)),
            pl.BlockSpec((1, d), lambda s: (0, 0)),
            col_block(0), col_block(nj), col_block(2 * nj),
            col_block(3 * nj), col_block(4 * nj),
            pl.BlockSpec((CONV_K, MIX_COLS), lambda s: (0, s % nj)),
            pl.BlockSpec((1, MIX_COLS), lambda s: (0, s % nj)),
            pl.BlockSpec((1, MIX_COLS), lambda s: (0, s % nj)),
            pl.BlockSpec((heads_per_step, CHUNK, CHUNK), lambda s: (s % nj, 0, 0)),
            pl.BlockSpec((None, CHUNK, heads_per_step), lambda s: (s % nj, 0, 0)),
            pl.BlockSpec((MIX_COLS, d), lambda s: (prev_j(s), 0)),
            pl.BlockSpec((MIX_COLS, d), lambda s: (nj + prev_j(s), 0)),
        ],
        out_specs=pl.BlockSpec((tm, d), lambda s: (prev_tile(s), 0)),
        out_shape=jax.ShapeDtypeStruct((m, d), F32),
        scratch_shapes=[pltpu.VMEM((tm, d), BF16), pltpu.VMEM((2, tm, 2 * MIX_COLS), BF16),
                        pltpu.VMEM((nj, SUBLANES, MIX_COLS), F32)],
        compiler_params=pltpu.CompilerParams(
            dimension_semantics=("arbitrary",), vmem_limit_bytes=_vmem_limit(vmem)),
        name="mix",
    )(h, g, w_in, w_in, w_in, w_in, w_in, conv_w, conv_b, g_v, w_s, b_s_t, w_out, w_out)


def _kv_kernel(mem_ref, g_ref, wk_ref, wv_ref, k_ref, v_ref, mn_ref):
    @pl.when(pl.program_id(0) == 0)
    def _():
        mn_ref[...] = _rmsnorm(mem_ref[...], g_ref[...]).astype(BF16)

    mn = mn_ref[...]
    k_ref[...] = _dot(mn, wk_ref[...].astype(BF16)).astype(BF16)
    v_ref[...] = _dot(mn, wv_ref[...].astype(BF16)).astype(BF16)


def _kv(mem, g, w_k, w_v, *, tn):
    m, d = mem.shape
    vmem = (2 * m * d * 4 + m * d * 2 + 2 * 2 * d * tn * w_k.dtype.itemsize
            + 2 * 2 * m * tn * 2)
    return pl.pallas_call(
        _kv_kernel,
        grid=(d // tn,),
        in_specs=[
            pl.BlockSpec((m, d), lambda n: (0, 0)),
            pl.BlockSpec((1, d), lambda n: (0, 0)),
            pl.BlockSpec((d, tn), lambda n: (0, n)),
            pl.BlockSpec((d, tn), lambda n: (0, n)),
        ],
        out_specs=[pl.BlockSpec((m, tn), lambda n: (0, n))] * 2,
        out_shape=[jax.ShapeDtypeStruct((m, d), BF16)] * 2,
        scratch_shapes=[pltpu.VMEM((m, d), BF16)],
        compiler_params=pltpu.CompilerParams(
            dimension_semantics=("arbitrary",), vmem_limit_bytes=_vmem_limit(vmem)),
        name="kv",
    )(mem, g, w_k, w_v)


def _xattn_kernel(h_ref, g_ref, wq_ref, k_ref, v_ref, wo_ref, o_ref, hn_ref, q_ref, *, n_steps):
    s = pl.program_id(0)
    cur = lax.rem(s, 2)
    prev = 1 - cur

    @pl.when(s == 0)
    def _():
        q_ref[prev] = jnp.zeros(q_ref.shape[1:], BF16)
        o_ref[...] = jnp.zeros_like(o_ref)

    @pl.when((lax.rem(s, XA_HEADS) == 0) & (s < n_steps))
    def _():
        hn_ref[...] = _rmsnorm(h_ref[...], g_ref[...]).astype(BF16)

    @pl.when(lax.rem(s, XA_HEADS) == 1)
    def _():
        o_ref[...] = h_ref[...]

    q_prev = q_ref[prev]
    scores = lax.dot_general(q_prev, k_ref[...], (((1,), (1,)), ((), ())),
                             preferred_element_type=F32)
    q_ref[cur] = _dot(hn_ref[...], wq_ref[...].astype(BF16)).astype(BF16)
    scores = scores * (q_prev.shape[-1] ** -0.5)
    p = jnp.exp(scores - jnp.max(scores, axis=-1, keepdims=True))
    p = p / jnp.sum(p, axis=-1, keepdims=True)
    o = _dot(p.astype(BF16), v_ref[...])
    o_ref[...] += _dot(o.astype(BF16), wo_ref[...].astype(BF16))


def _xattn(h, g, w_q, k, v, w_o, *, seq, n_mem, tm):
    m, d = h.shape
    hdim = d // XA_HEADS
    tiles_per_seq = seq // tm
    n_tiles = m // tm
    n_steps = n_tiles * XA_HEADS
    assert m % tm == 0 and seq % tm == 0

    def cur_tile(s):
        return jnp.minimum(s // XA_HEADS, n_tiles - 1)

    def prev_tile(s):
        return jnp.maximum(s - 1, 0) // XA_HEADS

    def prev_head(s):
        return jnp.maximum(s - 1, 0) % XA_HEADS

    vmem = (2 * 2 * tm * d * 4 + tm * d * 2 + 2 * tm * hdim * 2
            + 2 * 2 * d * hdim * w_q.dtype.itemsize + 2 * 2 * n_mem * hdim * 2)
    return pl.pallas_call(
        functools.partial(_xattn_kernel, n_steps=n_steps),
        grid=(n_steps + 1,),
        in_specs=[
            pl.BlockSpec((tm, d), lambda s: (cur_tile(s), 0)),
            pl.BlockSpec((1, d), lambda s: (0, 0)),
            pl.BlockSpec((d, hdim), lambda s: (0, s % XA_HEADS)),
            pl.BlockSpec((n_mem, hdim), lambda s: (prev_tile(s) // tiles_per_seq, prev_head(s))),
            pl.BlockSpec((n_mem, hdim), lambda s: (prev_tile(s) // tiles_per_seq, prev_head(s))),
            pl.BlockSpec((hdim, d), lambda s: (prev_head(s), 0)),
        ],
        out_specs=pl.BlockSpec((tm, d), lambda s: (prev_tile(s), 0)),
        out_shape=jax.ShapeDtypeStruct((m, d), F32),
        scratch_shapes=[pltpu.VMEM((tm, d), BF16), pltpu.VMEM((2, tm, hdim), BF16)],
        compiler_params=pltpu.CompilerParams(
            dimension_semantics=("arbitrary",), vmem_limit_bytes=_vmem_limit(vmem)),
        name="xattn",
    )(h, g, w_q, k, v, w_o)


def kernel(x, mem, g_ffn1, w_ffn1_in, w_ffn1_out, g_mix, w_mix_in, conv_w, conv_b, g_gm_v,
           w_spatial, b_spatial, w_mix_out, g_xattn, g_mem, w_xq, w_xk, w_xv, w_xo, g_ffn2,
           w_ffn2_in, w_ffn2_out, g_final):
    b, s, d = x.shape
    n_mem = mem.shape[1]
    depth = g_ffn1.shape[0]
    assert depth >= 1
    heads_per_step = MIX_COLS // GROUP_DIM
    ffn_tiles = dict(tm=2048, tf=256)
    tm = 512

    def row(v):
        return v.reshape(1, -1)

    h = x.reshape(b * s, d)
    mem2 = mem.reshape(b * n_mem, d)
    g_fin = row(g_final)
    for l in range(depth):
        last = l == depth - 1
        h = _ffn(h, row(g_ffn1[l]), w_ffn1_in[l], w_ffn1_out[l], g_fin,
                 final_norm=False, **ffn_tiles)
        b_s_t = b_spatial[l].reshape(-1, heads_per_step, CHUNK).transpose(0, 2, 1)
        h = _mix(h, row(g_mix[l]), w_mix_in[l].astype(BF16), conv_w[l], row(conv_b[l]),
                 row(g_gm_v[l]), w_spatial[l], b_s_t, w_mix_out[l].astype(BF16), seq=s, tm=tm)
        k, v = _kv(mem2, row(g_mem[l]), w_xk[l], w_xv[l], tn=512)
        h = _xattn(h, row(g_xattn[l]), w_xq[l].astype(BF16), k, v, w_xo[l].astype(BF16),
                   seq=s, n_mem=n_mem, tm=tm)
        h = _ffn(h, row(g_ffn2[l]), w_ffn2_in[l], w_ffn2_out[l], g_fin,
                 final_norm=last, **ffn_tiles)
    return h.reshape(b, s, d)
```

```python
import functools

import jax
import jax.numpy as jnp
from jax import lax
from jax.experimental import pallas as pl
from jax.experimental.pallas import tpu as pltpu

GROUP_DIM = 128
CHUNK = 128
CONV_K = 3
XA_HEADS = 4
EPS = 1e-6

V7X_VMEM_BYTES = 64 * 1024 * 1024
SUBLANES = 8
MIX_COLS = 256

BF16 = jnp.bfloat16
F32 = jnp.float32


def _rmsnorm(x, g):
    y = x * lax.rsqrt(jnp.mean(x * x, axis=-1, keepdims=True) + EPS)
    return y * g


_dot = functools.partial(jnp.dot, preferred_element_type=F32)


def _vmem_limit(nbytes):
    return min(int(nbytes * 1.25) + (8 << 20), V7X_VMEM_BYTES - (4 << 20))


def _ffn_kernel(x_hbm, g_ref, wg_ref, wu_ref, wo_ref, gf_ref, o_hbm, buf_ref, xn_ref, in_sem,
                out_sem, *, final_norm):
    i = pl.program_id(0)
    k = pl.program_id(1)
    n_tiles = pl.num_programs(0)
    nk = pl.num_programs(1)
    tm = xn_ref.shape[0]
    slot = lax.rem(i, 2)
    other = 1 - slot

    def load(tile, dst):
        return pltpu.make_async_copy(x_hbm.at[pl.ds(tile * tm, tm)], buf_ref.at[dst], in_sem)

    def store(tile, src):
        return pltpu.make_async_copy(buf_ref.at[src], o_hbm.at[pl.ds(tile * tm, tm)], out_sem)

    @pl.when(k == 0)
    def _():
        @pl.when(i == 0)
        def _():
            load(0, 0).start()

        load(i, slot).wait()
        xn_ref[...] = _rmsnorm(buf_ref[slot], g_ref[...]).astype(BF16)

        @pl.when(i > 0)
        def _():
            store(i - 1, other).start()

    @pl.when(k == nk // 2)
    def _():
        @pl.when(i > 0)
        def _():
            store(i - 1, other).wait()

        @pl.when(i + 1 < n_tiles)
        def _():
            load(i + 1, other).start()

    xn = xn_ref[...]
    gate = _dot(xn, wg_ref[...].astype(BF16))
    up = _dot(xn, wu_ref[...].astype(BF16))
    hidden = (0.5 * gate * jax.nn.sigmoid(gate) * up).astype(BF16)
    buf_ref[slot] += _dot(hidden, wo_ref[...].astype(BF16))

    @pl.when(k == nk - 1)
    def _():
        if final_norm:
            buf_ref[slot] = _rmsnorm(buf_ref[slot], gf_ref[...])

        @pl.when(i == n_tiles - 1)
        def _():
            store(i, slot).start()
            store(i, slot).wait()


def _ffn(x, g, w_in, w_out, g_final, *, final_norm, tm, tf):
    m, d = x.shape
    d_ff = w_out.shape[0]
    nk = d_ff // tf
    assert m % tm == 0 and d_ff % tf == 0 and nk >= 2
    vmem = (2 * tm * d * 4
            + tm * d * 2
            + 2 * 3 * d * tf * w_in.dtype.itemsize)
    return pl.pallas_call(
        functools.partial(_ffn_kernel, final_norm=final_norm),
        grid=(m // tm, nk),
        in_specs=[
            pl.BlockSpec(memory_space=pl.ANY),
            pl.BlockSpec((1, d), lambda i, k: (0, 0)),
            pl.BlockSpec((d, tf), lambda i, k: (0, k)),
            pl.BlockSpec((d, tf), lambda i, k: (0, nk + k)),
            pl.BlockSpec((tf, d), lambda i, k: (k, 0)),
            pl.BlockSpec((1, d), lambda i, k: (0, 0)),
        ],
        out_specs=pl.BlockSpec(memory_space=pl.ANY),
        out_shape=jax.ShapeDtypeStruct((m, d), F32),
        scratch_shapes=[pltpu.VMEM((2, tm, d), F32), pltpu.VMEM((tm, d), BF16),
                        pltpu.SemaphoreType.DMA(()), pltpu.SemaphoreType.DMA(())],
        compiler_params=pltpu.CompilerParams(
            dimension_semantics=("arbitrary", "arbitrary"),
            vmem_limit_bytes=_vmem_limit(vmem)),
        name="ffn_final" if final_norm else "ffn",
    )(x, g, w_in, w_in, w_out, g_final)


def _stream_row_tiles(s, per, n_tiles, h_hbm, o_hbm, buf_ref, in_sem, out_sem):
    tm = buf_ref.shape[1]
    t = s // per
    j = lax.rem(s, per)

    def load(tile):
        return pltpu.make_async_copy(h_hbm.at[pl.ds(tile * tm, tm)],
                                     buf_ref.at[lax.rem(tile, 2)], in_sem)

    def store(tile):
        return pltpu.make_async_copy(buf_ref.at[lax.rem(tile, 2)],
                                     o_hbm.at[pl.ds(tile * tm, tm)], out_sem)

    @pl.when(s == 0)
    def _():
        load(0).start()

    @pl.when((j == 0) & (t < n_tiles))
    def _():
        load(t).wait()

    @pl.when((j == 1) & (t >= 1))
    def _():
        store(t - 1).start()

    @pl.when(j == 2)
    def _():
        @pl.when(t >= 1)
        def _():
            store(t - 1).wait()

        @pl.when(t + 1 < n_tiles)
        def _():
            load(t + 1).start()

    return store


def _mix_kernel(h_hbm, g_ref, wb_ref, wc_ref, wh_ref, wu_ref, wv_ref, cw_ref, cb_ref,
                gv_ref, ws_ref, bs_ref, woc_ref, wog_ref, o_hbm, buf_ref, hn_ref, y_ref, carry_ref,
                in_sem, out_sem, *, nj, n_tiles, tiles_per_seq):
    s = pl.program_id(0)
    j = lax.rem(s, nj)
    cur = lax.rem(s, 2)
    prev = 1 - cur
    tm = hn_ref.shape[0]
    tile_slot = lax.rem(s // nj, 2)
    prev_slot = lax.rem(jnp.maximum(s - 1, 0) // nj, 2)
    store = _stream_row_tiles(s, nj, n_tiles, h_hbm, o_hbm, buf_ref, in_sem, out_sem)

    @pl.when(s == 0)
    def _():
        y_ref[prev] = jnp.zeros(y_ref.shape[1:], BF16)

    @pl.when((j == 0) & (s < n_tiles * nj))
    def _():
        hn_ref[...] = _rmsnorm(buf_ref[tile_slot], g_ref[...]).astype(BF16)

    @pl.when(lax.rem(s // nj, tiles_per_seq) == 0)
    def _():
        carry_ref[j] = jnp.zeros((SUBLANES, MIX_COLS), F32)

    w_in = jnp.concatenate([r[...].astype(BF16) for r in (wv_ref, wu_ref, wc_ref, wh_ref, wb_ref)],
                           axis=1)
    proj = _dot(hn_ref[...], w_in)
    v, u, gate_c, h_c, gate_b = (proj[:, n * MIX_COLS:(n + 1) * MIX_COLS] for n in range(5))

    y_prev = y_ref[prev]
    buf_ref[prev_slot] += _dot(y_prev[:, :MIX_COLS], woc_ref[...].astype(BF16))

    gv = gv_ref[...]
    bs = bs_ref[...]
    tri = (lax.broadcasted_iota(jnp.int32, (CHUNK, CHUNK), 0)
           >= lax.broadcasted_iota(jnp.int32, (CHUNK, CHUNK), 1))
    sgs = []
    for hh in range(MIX_COLS // GROUP_DIM):
        lanes = slice(hh * GROUP_DIM, (hh + 1) * GROUP_DIM)
        vh = _rmsnorm(v[:, lanes], gv[:, lanes]).astype(BF16)
        w = jnp.where(tri, ws_ref[hh], 0.0).astype(BF16)
        sgs.append([_dot(w, vh[c * CHUNK:(c + 1) * CHUNK]) for c in range(tm // CHUNK)])

    buf_ref[prev_slot] += _dot(y_prev[:, MIX_COLS:], wog_ref[...].astype(BF16))

    for hh, per_chunk in enumerate(sgs):
        lanes = slice(hh * GROUP_DIM, (hh + 1) * GROUP_DIM)
        for c, sg in enumerate(per_chunk):
            rows = slice(c * CHUNK, (c + 1) * CHUNK)
            y_ref[cur, rows, MIX_COLS + hh * GROUP_DIM:MIX_COLS + (hh + 1) * GROUP_DIM] = (
                u[rows, lanes] * (sg + bs[:, hh:hh + 1])).astype(BF16)

    z = gate_c * h_c
    tail = carry_ref[j]
    carry_ref[j] = z[tm - SUBLANES:, :]
    cw = cw_ref[...]
    cb = cb_ref[...]
    row = lax.broadcasted_iota(jnp.int32, (SUBLANES, MIX_COLS), 0)
    z1 = pltpu.roll(z, 1, 0)
    z2 = pltpu.roll(z, 2, 0)
    head1 = jnp.where(row < 1, pltpu.roll(tail, 1, 0), z1[:SUBLANES])
    head2 = jnp.where(row < 2, pltpu.roll(tail, 2, 0), z2[:SUBLANES])
    z1 = jnp.concatenate([head1, z1[SUBLANES:]], axis=0)
    z2 = jnp.concatenate([head2, z2[SUBLANES:]], axis=0)
    y_ref[cur, :, :MIX_COLS] = (
        gate_b * (cb + cw[2:3] * z + cw[1:2] * z1 + cw[0:1] * z2)).astype(BF16)

    @pl.when(s == n_tiles * nj)
    def _():
        store(n_tiles - 1).start()
        store(n_tiles - 1).wait()


def _mix(h, g, w_in, conv_w, conv_b, g_v, w_s, b_s_t, w_out, *, seq, tm):
    m, d = h.shape
    width = w_out.shape[0]
    conv_width = width // 2
    nj = conv_width // MIX_COLS
    heads_per_step = MIX_COLS // GROUP_DIM
    n_tiles = m // tm
    n_steps = n_tiles * nj
    assert m % tm == 0 and seq % tm == 0 and tm % CHUNK == 0 and nj >= 3

    def prev_j(s):
        return jnp.maximum(s - 1, 0) % nj

    def col_block(offset):
        return pl.BlockSpec((d, MIX_COLS), lambda s: (0, offset + s % nj))

    vmem = (2 * tm * d * 4 + tm * d * 2 + 2 * tm * 2 * MIX_COLS * 2
            + 2 * 5 * d * MIX_COLS * w_in.dtype.itemsize
            + 2 * 2 * MIX_COLS * d * w_out.dtype.itemsize)
    return pl.pallas_call(
        functools.partial(_mix_kernel, nj=nj, n_tiles=n_tiles, tiles_per_seq=seq // tm),
        grid=(n_steps + 1,),
        in_specs=[
            pl.BlockSpec(memory_space=pl.ANY),
            pl.BlockSpec((1, d), lambda s: (0, 0)),
            col_block(0), col_block(nj), col_block(2 * nj),
            col_block(3 * nj), col_block(4 * nj),
            pl.BlockSpec((CONV_K, MIX_COLS), lambda s: (0, s % nj)),
            pl.BlockSpec((1, MIX_COLS), lambda s: (0, s % nj)),
            pl.BlockSpec((1, MIX_COLS), lambda s: (0, s % nj)),
            pl.BlockSpec((heads_per_step, CHUNK, CHUNK), lambda s: (s % nj, 0, 0)),
            pl.BlockSpec((None, CHUNK, heads_per_step), lambda s: (s % nj, 0, 0)),
            pl.BlockSpec((MIX_COLS, d), lambda s: (prev_j(s), 0)),
            pl.BlockSpec((MIX_COLS, d), lambda s: (nj + prev_j(s), 0)),
        ],
        out_specs=pl.BlockSpec(memory_space=pl.ANY),
        out_shape=jax.ShapeDtypeStruct((m, d), F32),
        scratch_shapes=[pltpu.VMEM((2, tm, d), F32), pltpu.VMEM((tm, d), BF16),
                        pltpu.VMEM((2, tm, 2 * MIX_COLS), BF16),
                        pltpu.VMEM((nj, SUBLANES, MIX_COLS), F32),
                        pltpu.SemaphoreType.DMA(()), pltpu.SemaphoreType.DMA(())],
        compiler_params=pltpu.CompilerParams(
            dimension_semantics=("arbitrary",), vmem_limit_bytes=_vmem_limit(vmem)),
        name="mix",
    )(h, g, w_in, w_in, w_in, w_in, w_in, conv_w, conv_b, g_v, w_s, b_s_t, w_out, w_out)


def _kv_kernel(mem_ref, g_ref, wk_ref, wv_ref, k_ref, v_ref, mn_ref):
    @pl.when(pl.program_id(0) == 0)
    def _():
        mn_ref[...] = _rmsnorm(mem_ref[...], g_ref[...]).astype(BF16)

    mn = mn_ref[...]
    k_ref[...] = _dot(mn, wk_ref[...].astype(BF16)).astype(BF16)
    v_ref[...] = _dot(mn, wv_ref[...].astype(BF16)).astype(BF16)


def _kv(mem, g, w_k, w_v, *, tn):
    m, d = mem.shape
    vmem = (2 * m * d * 4 + m * d * 2 + 2 * 2 * d * tn * w_k.dtype.itemsize
            + 2 * 2 * m * tn * 2)
    return pl.pallas_call(
        _kv_kernel,
        grid=(d // tn,),
        in_specs=[
            pl.BlockSpec((m, d), lambda n: (0, 0)),
            pl.BlockSpec((1, d), lambda n: (0, 0)),
            pl.BlockSpec((d, tn), lambda n: (0, n)),
            pl.BlockSpec((d, tn), lambda n: (0, n)),
        ],
        out_specs=[pl.BlockSpec((m, tn), lambda n: (0, n))] * 2,
        out_shape=[jax.ShapeDtypeStruct((m, d), BF16)] * 2,
        scratch_shapes=[pltpu.VMEM((m, d), BF16)],
        compiler_params=pltpu.CompilerParams(
            dimension_semantics=("arbitrary",), vmem_limit_bytes=_vmem_limit(vmem)),
        name="kv",
    )(mem, g, w_k, w_v)


def _xattn_kernel(h_hbm, g_ref, wq_ref, k_ref, v_ref, wo_ref, o_hbm, buf_ref, hn_ref, q_ref,
                  in_sem, out_sem, *, n_tiles):
    s = pl.program_id(0)
    cur = lax.rem(s, 2)
    prev = 1 - cur
    tile_slot = lax.rem(s // XA_HEADS, 2)
    prev_slot = lax.rem(jnp.maximum(s - 1, 0) // XA_HEADS, 2)
    store = _stream_row_tiles(s, XA_HEADS, n_tiles, h_hbm, o_hbm, buf_ref, in_sem, out_sem)

    @pl.when(s == 0)
    def _():
        q_ref[prev] = jnp.zeros(q_ref.shape[1:], BF16)

    @pl.when((lax.rem(s, XA_HEADS) == 0) & (s < n_tiles * XA_HEADS))
    def _():
        hn_ref[...] = _rmsnorm(buf_ref[tile_slot], g_ref[...]).astype(BF16)

    q_prev = q_ref[prev]
    scores = lax.dot_general(q_prev, k_ref[...], (((1,), (1,)), ((), ())),
                             preferred_element_type=F32)
    q_ref[cur] = _dot(hn_ref[...], wq_ref[...].astype(BF16)).astype(BF16)
    scores = scores * (q_prev.shape[-1] ** -0.5)
    p = jnp.exp(scores - jnp.max(scores, axis=-1, keepdims=True))
    p = p / jnp.sum(p, axis=-1, keepdims=True)
    o = _dot(p.astype(BF16), v_ref[...])
    o = jnp.where(s > 0, o, 0.0)
    buf_ref[prev_slot] += _dot(o.astype(BF16), wo_ref[...].astype(BF16))

    @pl.when(s == n_tiles * XA_HEADS)
    def _():
        store(n_tiles - 1).start()
        store(n_tiles - 1).wait()


def _xattn(h, g, w_q, k, v, w_o, *, seq, n_mem, tm):
    m, d = h.shape
    hdim = d // XA_HEADS
    tiles_per_seq = seq // tm
    n_tiles = m // tm
    n_steps = n_tiles * XA_HEADS
    assert m % tm == 0 and seq % tm == 0 and XA_HEADS >= 3

    def prev_tile(s):
        return jnp.maximum(s - 1, 0) // XA_HEADS

    def prev_head(s):
        return jnp.maximum(s - 1, 0) % XA_HEADS

    vmem = (2 * tm * d * 4 + tm * d * 2 + 2 * tm * hdim * 2
            + 2 * 2 * d * hdim * w_q.dtype.itemsize + 2 * 2 * n_mem * hdim * 2)
    return pl.pallas_call(
        functools.partial(_xattn_kernel, n_tiles=n_tiles),
        grid=(n_steps + 1,),
        in_specs=[
            pl.BlockSpec(memory_space=pl.ANY),
            pl.BlockSpec((1, d), lambda s: (0, 0)),
            pl.BlockSpec((d, hdim), lambda s: (0, s % XA_HEADS)),
            pl.BlockSpec((n_mem, hdim), lambda s: (prev_tile(s) // tiles_per_seq, prev_head(s))),
            pl.BlockSpec((n_mem, hdim), lambda s: (prev_tile(s) // tiles_per_seq, prev_head(s))),
            pl.BlockSpec((hdim, d), lambda s: (prev_head(s), 0)),
        ],
        out_specs=pl.BlockSpec(memory_space=pl.ANY),
        out_shape=jax.ShapeDtypeStruct((m, d), F32),
        scratch_shapes=[pltpu.VMEM((2, tm, d), F32), pltpu.VMEM((tm, d), BF16),
                        pltpu.VMEM((2, tm, hdim), BF16),
                        pltpu.SemaphoreType.DMA(()), pltpu.SemaphoreType.DMA(())],
        compiler_params=pltpu.CompilerParams(
            dimension_semantics=("arbitrary",), vmem_limit_bytes=_vmem_limit(vmem)),
        name="xattn",
    )(h, g, w_q, k, v, w_o)


def kernel(x, mem, g_ffn1, w_ffn1_in, w_ffn1_out, g_mix, w_mix_in, conv_w, conv_b, g_gm_v,
           w_spatial, b_spatial, w_mix_out, g_xattn, g_mem, w_xq, w_xk, w_xv, w_xo, g_ffn2,
           w_ffn2_in, w_ffn2_out, g_final):
    b, s, d = x.shape
    n_mem = mem.shape[1]
    depth = g_ffn1.shape[0]
    assert depth >= 1
    heads_per_step = MIX_COLS // GROUP_DIM
    ffn_tiles = dict(tm=2048, tf=256)
    tm = 1024

    def row(v):
        return v.reshape(1, -1)

    h = x.reshape(b * s, d)
    mem2 = mem.reshape(b * n_mem, d)
    g_fin = row(g_final)
    for l in range(depth):
        last = l == depth - 1
        h = _ffn(h, row(g_ffn1[l]), w_ffn1_in[l], w_ffn1_out[l], g_fin,
                 final_norm=False, **ffn_tiles)
        b_s_t = b_spatial[l].reshape(-1, heads_per_step, CHUNK).transpose(0, 2, 1)
        h = _mix(h, row(g_mix[l]), w_mix_in[l], conv_w[l], row(conv_b[l]), row(g_gm_v[l]),
                 w_spatial[l], b_s_t, w_mix_out[l], seq=s, tm=tm)
        k, v = _kv(mem2, row(g_mem[l]), w_xk[l], w_xv[l], tn=512)
        h = _xattn(h, row(g_xattn[l]), w_xq[l], k, v, w_xo[l], seq=s, n_mem=n_mem, tm=tm)
        h = _ffn(h, row(g_ffn2[l]), w_ffn2_in[l], w_ffn2_out[l], g_fin,
                 final_norm=last, **ffn_tiles)
    return h.reshape(b, s, d)
```

```python
import functools

import jax
import jax.numpy as jnp
from jax import lax
from jax.experimental import pallas as pl
from jax.experimental.pallas import tpu as pltpu

GROUP_DIM = 128
CHUNK = 128
CONV_K = 3
XA_HEADS = 4
EPS = 1e-6

V7X_VMEM_BYTES = 64 * 1024 * 1024
SUBLANES = 8
MIX_COLS = 256

BF16 = jnp.bfloat16
F32 = jnp.float32


def _rmsnorm(x, g):
    y = x * lax.rsqrt(jnp.mean(x * x, axis=-1, keepdims=True) + EPS)
    return y * g


_dot = functools.partial(jnp.dot, preferred_element_type=F32)


def _vmem_limit(nbytes):
    return min(int(nbytes * 1.25) + (8 << 20), V7X_VMEM_BYTES - (4 << 20))


def _ffn_kernel(x_hbm, g_ref, win_hbm, wout_hbm, gf_ref, o_hbm, buf_ref, xn_ref, wg_ref, wu_ref,
                wo_ref, in_sem, out_sem, w_sem, *, final_norm, blocks_per_step):
    i = pl.program_id(0)
    kk = pl.program_id(1)
    n_tiles = pl.num_programs(0)
    tm, d = xn_ref.shape
    tf = wo_ref.shape[1]
    d_ff = wout_hbm.shape[0]
    nk = d_ff // tf
    nkk = nk // blocks_per_step
    slot = lax.rem(i, 2)
    other = 1 - slot

    def load(tile, dst):
        return pltpu.make_async_copy(x_hbm.at[pl.ds(tile * tm, tm)], buf_ref.at[dst], in_sem)

    def store(tile, src):
        return pltpu.make_async_copy(buf_ref.at[src], o_hbm.at[pl.ds(tile * tm, tm)], out_sem)

    def weight_copies(k, ws):
        cols = pl.ds(pl.multiple_of(k * tf, tf), tf)
        up_cols = pl.ds(pl.multiple_of(d_ff + k * tf, tf), tf)
        return (pltpu.make_async_copy(win_hbm.at[:, cols], wg_ref.at[ws], w_sem.at[0, ws]),
                pltpu.make_async_copy(win_hbm.at[:, up_cols], wu_ref.at[ws], w_sem.at[1, ws]),
                pltpu.make_async_copy(wout_hbm.at[cols, :], wo_ref.at[ws], w_sem.at[2, ws]))

    @pl.when(kk == 0)
    def _():
        @pl.when(i == 0)
        def _():
            load(0, 0).start()
            for c in weight_copies(0, 0):
                c.start()

        load(i, slot).wait()
        xn_ref[...] = _rmsnorm(buf_ref[slot], g_ref[...]).astype(BF16)

        @pl.when(i > 0)
        def _():
            store(i - 1, other).start()

    def swap_other_tile():
        @pl.when(i > 0)
        def _():
            store(i - 1, other).wait()

        @pl.when(i + 1 < n_tiles)
        def _():
            load(i + 1, other).start()

    xn = xn_ref[...]
    for u in range(blocks_per_step):
        k = kk * blocks_per_step + u
        if u == (blocks_per_step // 2 if nkk == 1 else 0):
            pl.when(kk == nkk // 2)(swap_other_tile)
        ws = lax.rem(i * nk + k, 2)
        for c in weight_copies(k, ws):
            c.wait()

        for c in weight_copies(lax.rem(k + 1, nk), 1 - ws):
            c.start()

        gate = _dot(xn, wg_ref[ws].astype(BF16))
        up = _dot(xn, wu_ref[ws].astype(BF16))
        hidden = (0.5 * gate * jax.nn.sigmoid(gate) * up).astype(BF16)
        buf_ref[slot] += _dot(hidden, wo_ref[ws].astype(BF16))

    @pl.when(kk == nkk - 1)
    def _():
        if final_norm:
            buf_ref[slot] = _rmsnorm(buf_ref[slot], gf_ref[...])

        @pl.when(i == n_tiles - 1)
        def _():
            store(i, slot).start()
            for c in weight_copies(0, lax.rem(n_tiles * nk, 2)):
                c.wait()
            store(i, slot).wait()


def _ffn(x, g, w_in, w_out, g_final, *, final_norm, tm, tf, blocks_per_step):
    m, d = x.shape
    d_ff = w_out.shape[0]
    nk = d_ff // tf
    assert m % tm == 0 and d_ff % tf == 0 and nk % blocks_per_step == 0 and nk >= 2
    vmem = (2 * tm * d * 4
            + tm * d * 2
            + 2 * 3 * d * tf * w_in.dtype.itemsize)
    return pl.pallas_call(
        functools.partial(_ffn_kernel, final_norm=final_norm, blocks_per_step=blocks_per_step),
        grid=(m // tm, nk // blocks_per_step),
        in_specs=[
            pl.BlockSpec(memory_space=pl.ANY),
            pl.BlockSpec((1, d), lambda i, k: (0, 0)),
            pl.BlockSpec(memory_space=pl.ANY),
            pl.BlockSpec(memory_space=pl.ANY),
            pl.BlockSpec((1, d), lambda i, k: (0, 0)),
        ],
        out_specs=pl.BlockSpec(memory_space=pl.ANY),
        out_shape=jax.ShapeDtypeStruct((m, d), F32),
        scratch_shapes=[pltpu.VMEM((2, tm, d), F32), pltpu.VMEM((tm, d), BF16),
                        pltpu.VMEM((2, d, tf), w_in.dtype), pltpu.VMEM((2, d, tf), w_in.dtype),
                        pltpu.VMEM((2, tf, d), w_out.dtype),
                        pltpu.SemaphoreType.DMA(()), pltpu.SemaphoreType.DMA(()),
                        pltpu.SemaphoreType.DMA((3, 2))],
        compiler_params=pltpu.CompilerParams(
            dimension_semantics=("arbitrary", "arbitrary"),
            vmem_limit_bytes=_vmem_limit(vmem)),
        name="ffn_final" if final_norm else "ffn",
    )(x, g, w_in, w_out, g_final)


def _stream_row_tiles(s, per, n_tiles, h_hbm, o_hbm, buf_ref, in_sem, out_sem):
    tm = buf_ref.shape[1]
    t = s // per
    j = lax.rem(s, per)

    def load(tile):
        return pltpu.make_async_copy(h_hbm.at[pl.ds(tile * tm, tm)],
                                     buf_ref.at[lax.rem(tile, 2)], in_sem)

    def store(tile):
        return pltpu.make_async_copy(buf_ref.at[lax.rem(tile, 2)],
                                     o_hbm.at[pl.ds(tile * tm, tm)], out_sem)

    @pl.when(s == 0)
    def _():
        load(0).start()

    @pl.when((j == 0) & (t < n_tiles))
    def _():
        load(t).wait()

    @pl.when((j == 1) & (t >= 1))
    def _():
        store(t - 1).start()

    @pl.when(j == 2)
    def _():
        @pl.when(t >= 1)
        def _():
            store(t - 1).wait()

        @pl.when(t + 1 < n_tiles)
        def _():
            load(t + 1).start()

    return store


def _mix_kernel(h_hbm, g_ref, wb_ref, wc_ref, wh_ref, wu_ref, wv_ref, cw_ref, cb_ref,
                gv_ref, ws_ref, bs_ref, woc_ref, wog_ref, o_hbm, buf_ref, hn_ref, y_ref, carry_ref,
                in_sem, out_sem, *, nj, n_tiles, tiles_per_seq):
    s = pl.program_id(0)
    j = lax.rem(s, nj)
    cur = lax.rem(s, 2)
    prev = 1 - cur
    tm = hn_ref.shape[0]
    tile_slot = lax.rem(s // nj, 2)
    prev_slot = lax.rem(jnp.maximum(s - 1, 0) // nj, 2)
    store = _stream_row_tiles(s, nj, n_tiles, h_hbm, o_hbm, buf_ref, in_sem, out_sem)

    @pl.when(s == 0)
    def _():
        y_ref[prev] = jnp.zeros(y_ref.shape[1:], BF16)

    @pl.when((j == 0) & (s < n_tiles * nj))
    def _():
        hn_ref[...] = _rmsnorm(buf_ref[tile_slot], g_ref[...]).astype(BF16)

    @pl.when(lax.rem(s // nj, tiles_per_seq) == 0)
    def _():
        carry_ref[j] = jnp.zeros((SUBLANES, MIX_COLS), F32)

    w_in = jnp.concatenate([r[...].astype(BF16) for r in (wv_ref, wu_ref, wc_ref, wh_ref, wb_ref)],
                           axis=1)
    proj = _dot(hn_ref[...], w_in)
    v, u, gate_c, h_c, gate_b = (proj[:, n * MIX_COLS:(n + 1) * MIX_COLS] for n in range(5))

    y_prev = y_ref[prev]
    buf_ref[prev_slot] += _dot(y_prev[:, :MIX_COLS], woc_ref[...].astype(BF16))

    gv = gv_ref[...]
    bs = bs_ref[...]
    tri = (lax.broadcasted_iota(jnp.int32, (CHUNK, CHUNK), 0)
           >= lax.broadcasted_iota(jnp.int32, (CHUNK, CHUNK), 1))
    sgs = []
    for hh in range(MIX_COLS // GROUP_DIM):
        lanes = slice(hh * GROUP_DIM, (hh + 1) * GROUP_DIM)
        vh = _rmsnorm(v[:, lanes], gv[:, lanes]).astype(BF16)
        w = jnp.where(tri, ws_ref[hh], 0.0).astype(BF16)
        sgs.append([_dot(w, vh[c * CHUNK:(c + 1) * CHUNK]) for c in range(tm // CHUNK)])

    buf_ref[prev_slot] += _dot(y_prev[:, MIX_COLS:], wog_ref[...].astype(BF16))

    for hh, per_chunk in enumerate(sgs):
        lanes = slice(hh * GROUP_DIM, (hh + 1) * GROUP_DIM)
        for c, sg in enumerate(per_chunk):
            rows = slice(c * CHUNK, (c + 1) * CHUNK)
            y_ref[cur, rows, MIX_COLS + hh * GROUP_DIM:MIX_COLS + (hh + 1) * GROUP_DIM] = (
                u[rows, lanes] * (sg + bs[:, hh:hh + 1])).astype(BF16)

    z = gate_c * h_c
    tail = carry_ref[j]
    carry_ref[j] = z[tm - SUBLANES:, :]
    cw = cw_ref[...]
    cb = cb_ref[...]
    row = lax.broadcasted_iota(jnp.int32, (SUBLANES, MIX_COLS), 0)
    z1 = pltpu.roll(z, 1, 0)
    z2 = pltpu.roll(z, 2, 0)
    head1 = jnp.where(row < 1, pltpu.roll(tail, 1, 0), z1[:SUBLANES])
    head2 = jnp.where(row < 2, pltpu.roll(tail, 2, 0), z2[:SUBLANES])
    z1 = jnp.concatenate([head1, z1[SUBLANES:]], axis=0)
    z2 = jnp.concatenate([head2, z2[SUBLANES:]], axis=0)
    y_ref[cur, :, :MIX_COLS] = (
        gate_b * (cb + cw[2:3] * z + cw[1:2] * z1 + cw[0:1] * z2)).astype(BF16)

    @pl.when(s == n_tiles * nj)
    def _():
        store(n_tiles - 1).start()
        store(n_tiles - 1).wait()


def _mix(h, g, w_in, conv_w, conv_b, g_v, w_s, b_s_t, w_out, *, seq, tm):
    m, d = h.shape
    width = w_out.shape[0]
    conv_width = width // 2
    nj = conv_width // MIX_COLS
    heads_per_step = MIX_COLS // GROUP_DIM
    n_tiles = m // tm
    n_steps = n_tiles * nj
    assert m % tm == 0 and seq % tm == 0 and tm % CHUNK == 0 and nj >= 3

    def prev_j(s):
        return jnp.maximum(s - 1, 0) % nj

    def col_block(offset):
        return pl.BlockSpec((d, MIX_COLS), lambda s: (0, offset + s % nj))

    vmem = (2 * tm * d * 4 + tm * d * 2 + 2 * tm * 2 * MIX_COLS * 2
            + 2 * 5 * d * MIX_COLS * w_in.dtype.itemsize
            + 2 * 2 * MIX_COLS * d * w_out.dtype.itemsize)
    return pl.pallas_call(
        functools.partial(_mix_kernel, nj=nj, n_tiles=n_tiles, tiles_per_seq=seq // tm),
        grid=(n_steps + 1,),
        in_specs=[
            pl.BlockSpec(memory_space=pl.ANY),
            pl.BlockSpec((1, d), lambda s: (0, 0)),
            col_block(0), col_block(nj), col_block(2 * nj),
            col_block(3 * nj), col_block(4 * nj),
            pl.BlockSpec((CONV_K, MIX_COLS), lambda s: (0, s % nj)),
            pl.BlockSpec((1, MIX_COLS), lambda s: (0, s % nj)),
            pl.BlockSpec((1, MIX_COLS), lambda s: (0, s % nj)),
            pl.BlockSpec((heads_per_step, CHUNK, CHUNK), lambda s: (s % nj, 0, 0)),
            pl.BlockSpec((None, CHUNK, heads_per_step), lambda s: (s % nj, 0, 0)),
            pl.BlockSpec((MIX_COLS, d), lambda s: (prev_j(s), 0)),
            pl.BlockSpec((MIX_COLS, d), lambda s: (nj + prev_j(s), 0)),
        ],
        out_specs=pl.BlockSpec(memory_space=pl.ANY),
        out_shape=jax.ShapeDtypeStruct((m, d), F32),
        scratch_shapes=[pltpu.VMEM((2, tm, d), F32), pltpu.VMEM((tm, d), BF16),
                        pltpu.VMEM((2, tm, 2 * MIX_COLS), BF16),
                        pltpu.VMEM((nj, SUBLANES, MIX_COLS), F32),
                        pltpu.SemaphoreType.DMA(()), pltpu.SemaphoreType.DMA(())],
        compiler_params=pltpu.CompilerParams(
            dimension_semantics=("arbitrary",), vmem_limit_bytes=_vmem_limit(vmem)),
        name="mix",
    )(h, g, w_in, w_in, w_in, w_in, w_in, conv_w, conv_b, g_v, w_s, b_s_t, w_out, w_out)


def _kv_kernel(mem_ref, g_ref, wk_ref, wv_ref, k_ref, v_ref, mn_ref):
    @pl.when(pl.program_id(0) == 0)
    def _():
        mn_ref[...] = _rmsnorm(mem_ref[...], g_ref[...]).astype(BF16)

    mn = mn_ref[...]
    k_ref[...] = _dot(mn, wk_ref[...].astype(BF16)).astype(BF16)
    v_ref[...] = _dot(mn, wv_ref[...].astype(BF16)).astype(BF16)


def _kv(mem, g, w_k, w_v, *, tn):
    m, d = mem.shape
    vmem = (2 * m * d * 4 + m * d * 2 + 2 * 2 * d * tn * w_k.dtype.itemsize
            + 2 * 2 * m * tn * 2)
    return pl.pallas_call(
        _kv_kernel,
        grid=(d // tn,),
        in_specs=[
            pl.BlockSpec((m, d), lambda n: (0, 0)),
            pl.BlockSpec((1, d), lambda n: (0, 0)),
            pl.BlockSpec((d, tn), lambda n: (0, n)),
            pl.BlockSpec((d, tn), lambda n: (0, n)),
        ],
        out_specs=[pl.BlockSpec((m, tn), lambda n: (0, n))] * 2,
        out_shape=[jax.ShapeDtypeStruct((m, d), BF16)] * 2,
        scratch_shapes=[pltpu.VMEM((m, d), BF16)],
        compiler_params=pltpu.CompilerParams(
            dimension_semantics=("arbitrary",), vmem_limit_bytes=_vmem_limit(vmem)),
        name="kv",
    )(mem, g, w_k, w_v)


def _xattn_kernel(h_hbm, g_ref, wq_ref, k_ref, v_ref, wo_ref, o_hbm, buf_ref, hn_ref, q_ref,
                  in_sem, out_sem, *, n_tiles):
    s = pl.program_id(0)
    cur = lax.rem(s, 2)
    prev = 1 - cur
    tile_slot = lax.rem(s // XA_HEADS, 2)
    prev_slot = lax.rem(jnp.maximum(s - 1, 0) // XA_HEADS, 2)
    store = _stream_row_tiles(s, XA_HEADS, n_tiles, h_hbm, o_hbm, buf_ref, in_sem, out_sem)

    @pl.when(s == 0)
    def _():
        q_ref[prev] = jnp.zeros(q_ref.shape[1:], BF16)

    @pl.when((lax.rem(s, XA_HEADS) == 0) & (s < n_tiles * XA_HEADS))
    def _():
        hn_ref[...] = _rmsnorm(buf_ref[tile_slot], g_ref[...]).astype(BF16)

    q_prev = q_ref[prev]
    scores = lax.dot_general(q_prev, k_ref[...], (((1,), (1,)), ((), ())),
                             preferred_element_type=F32)
    q_ref[cur] = _dot(hn_ref[...], wq_ref[...].astype(BF16)).astype(BF16)
    scores = scores * (q_prev.shape[-1] ** -0.5)
    p = jnp.exp(scores - jnp.max(scores, axis=-1, keepdims=True))
    p = p / jnp.sum(p, axis=-1, keepdims=True)
    o = _dot(p.astype(BF16), v_ref[...])
    o = jnp.where(s > 0, o, 0.0)
    buf_ref[prev_slot] += _dot(o.astype(BF16), wo_ref[...].astype(BF16))

    @pl.when(s == n_tiles * XA_HEADS)
    def _():
        store(n_tiles - 1).start()
        store(n_tiles - 1).wait()


def _xattn(h, g, w_q, k, v, w_o, *, seq, n_mem, tm):
    m, d = h.shape
    hdim = d // XA_HEADS
    tiles_per_seq = seq // tm
    n_tiles = m // tm
    n_steps = n_tiles * XA_HEADS
    assert m % tm == 0 and seq % tm == 0 and XA_HEADS >= 3

    def prev_tile(s):
        return jnp.maximum(s - 1, 0) // XA_HEADS

    def prev_head(s):
        return jnp.maximum(s - 1, 0) % XA_HEADS

    vmem = (2 * tm * d * 4 + tm * d * 2 + 2 * tm * hdim * 2
            + 2 * 2 * d * hdim * w_q.dtype.itemsize + 2 * 2 * n_mem * hdim * 2)
    return pl.pallas_call(
        functools.partial(_xattn_kernel, n_tiles=n_tiles),
        grid=(n_steps + 1,),
        in_specs=[
            pl.BlockSpec(memory_space=pl.ANY),
            pl.BlockSpec((1, d), lambda s: (0, 0)),
            pl.BlockSpec((d, hdim), lambda s: (0, s % XA_HEADS)),
            pl.BlockSpec((n_mem, hdim), lambda s: (prev_tile(s) // tiles_per_seq, prev_head(s))),
            pl.BlockSpec((n_mem, hdim), lambda s: (prev_tile(s) // tiles_per_seq, prev_head(s))),
            pl.BlockSpec((hdim, d), lambda s: (prev_head(s), 0)),
        ],
        out_specs=pl.BlockSpec(memory_space=pl.ANY),
        out_shape=jax.ShapeDtypeStruct((m, d), F32),
        scratch_shapes=[pltpu.VMEM((2, tm, d), F32), pltpu.VMEM((tm, d), BF16),
                        pltpu.VMEM((2, tm, hdim), BF16),
                        pltpu.SemaphoreType.DMA(()), pltpu.SemaphoreType.DMA(())],
        compiler_params=pltpu.CompilerParams(
            dimension_semantics=("arbitrary",), vmem_limit_bytes=_vmem_limit(vmem)),
        name="xattn",
    )(h, g, w_q, k, v, w_o)


def kernel(x, mem, g_ffn1, w_ffn1_in, w_ffn1_out, g_mix, w_mix_in, conv_w, conv_b, g_gm_v,
           w_spatial, b_spatial, w_mix_out, g_xattn, g_mem, w_xq, w_xk, w_xv, w_xo, g_ffn2,
           w_ffn2_in, w_ffn2_out, g_final):
    b, s, d = x.shape
    n_mem = mem.shape[1]
    depth = g_ffn1.shape[0]
    assert depth >= 1
    heads_per_step = MIX_COLS // GROUP_DIM
    ffn_tiles = dict(tm=1024, tf=512, blocks_per_step=11)
    tm = 1024

    def row(v):
        return v.reshape(1, -1)

    h = x.reshape(b * s, d)
    mem2 = mem.reshape(b * n_mem, d)
    g_fin = row(g_final)
    for l in range(depth):
        last = l == depth - 1
        h = _ffn(h, row(g_ffn1[l]), w_ffn1_in[l], w_ffn1_out[l], g_fin,
                 final_norm=False, **ffn_tiles)
        b_s_t = b_spatial[l].reshape(-1, heads_per_step, CHUNK).transpose(0, 2, 1)
        h = _mix(h, row(g_mix[l]), w_mix_in[l], conv_w[l], row(conv_b[l]), row(g_gm_v[l]),
                 w_spatial[l], b_s_t, w_mix_out[l], seq=s, tm=tm)
        k, v = _kv(mem2, row(g_mem[l]), w_xk[l], w_xv[l], tn=512)
        h = _xattn(h, row(g_xattn[l]), w_xq[l], k, v, w_xo[l], seq=s, n_mem=n_mem, tm=tm)
        h = _ffn(h, row(g_ffn2[l]), w_ffn2_in[l], w_ffn2_out[l], g_fin,
                 final_norm=last, **ffn_tiles)
    return h.reshape(b, s, d)
```

```python
import functools

import jax
import jax.numpy as jnp
from jax import lax
from jax.experimental import pallas as pl
from jax.experimental.pallas import tpu as pltpu

GROUP_DIM = 128
CHUNK = 128
CONV_K = 3
XA_HEADS = 4
EPS = 1e-6

V7X_VMEM_BYTES = 64 * 1024 * 1024
SUBLANES = 8
MIX_COLS = 256

BF16 = jnp.bfloat16
F32 = jnp.float32


def _rmsnorm(x, g):
    y = x * lax.rsqrt(jnp.mean(x * x, axis=-1, keepdims=True) + EPS)
    return y * g


_dot = functools.partial(jnp.dot, preferred_element_type=F32)


def _vmem_limit(nbytes):
    return min(int(nbytes * 1.25) + (8 << 20), V7X_VMEM_BYTES - (4 << 20))


def _ffn_kernel(x_hbm, g_ref, win_hbm, wout_hbm, gf_ref, o_hbm, buf_ref, xn_ref, wg_ref, wu_ref,
                wo_ref, in_sem, out_sem, w_sem, *, final_norm, blocks_per_step, row_chunk):
    i = pl.program_id(0)
    kk = pl.program_id(1)
    n_tiles = pl.num_programs(0)
    tm, d = xn_ref.shape
    tf = wo_ref.shape[1]
    d_ff = wout_hbm.shape[0]
    nk = d_ff // tf
    nkk = nk // blocks_per_step
    slot = lax.rem(i, 2)
    other = 1 - slot

    def load(tile, dst):
        return pltpu.make_async_copy(x_hbm.at[pl.ds(tile * tm, tm)], buf_ref.at[dst], in_sem)

    def store(tile, src):
        return pltpu.make_async_copy(buf_ref.at[src], o_hbm.at[pl.ds(tile * tm, tm)], out_sem)

    def weight_copies(k, ws):
        cols = pl.ds(pl.multiple_of(k * tf, tf), tf)
        up_cols = pl.ds(pl.multiple_of(d_ff + k * tf, tf), tf)
        return (pltpu.make_async_copy(win_hbm.at[:, cols], wg_ref.at[ws], w_sem.at[0, ws]),
                pltpu.make_async_copy(win_hbm.at[:, up_cols], wu_ref.at[ws], w_sem.at[1, ws]),
                pltpu.make_async_copy(wout_hbm.at[cols, :], wo_ref.at[ws], w_sem.at[2, ws]))

    @pl.when(kk == 0)
    def _():
        @pl.when(i == 0)
        def _():
            load(0, 0).start()
            for c in weight_copies(0, 0):
                c.start()

        load(i, slot).wait()
        xn_ref[...] = _rmsnorm(buf_ref[slot], g_ref[...]).astype(BF16)

        @pl.when(i > 0)
        def _():
            store(i - 1, other).start()

    def swap_other_tile():
        @pl.when(i > 0)
        def _():
            store(i - 1, other).wait()

        @pl.when(i + 1 < n_tiles)
        def _():
            load(i + 1, other).start()

    for u in range(blocks_per_step):
        k = kk * blocks_per_step + u
        if u == (blocks_per_step // 2 if nkk == 1 else 0):
            pl.when(kk == nkk // 2)(swap_other_tile)
        ws = lax.rem(i * nk + k, 2)
        for c in weight_copies(k, ws):
            c.wait()

        for c in weight_copies(lax.rem(k + 1, nk), 1 - ws):
            c.start()

        w_gate, w_up, w_down = (r[ws].astype(BF16) for r in (wg_ref, wu_ref, wo_ref))
        for rows in (pl.ds(r * row_chunk, row_chunk) for r in range(tm // row_chunk)):
            xn = xn_ref[rows, :]
            gate = _dot(xn, w_gate)
            up = _dot(xn, w_up)
            hidden = (0.5 * gate * jax.nn.sigmoid(gate) * up).astype(BF16)
            buf_ref[slot, rows, :] += _dot(hidden, w_down)

    @pl.when(kk == nkk - 1)
    def _():
        if final_norm:
            buf_ref[slot] = _rmsnorm(buf_ref[slot], gf_ref[...])

        @pl.when(i == n_tiles - 1)
        def _():
            store(i, slot).start()
            for c in weight_copies(0, lax.rem(n_tiles * nk, 2)):
                c.wait()
            store(i, slot).wait()


def _ffn(x, g, w_in, w_out, g_final, *, final_norm, tm, tf, blocks_per_step, row_chunk):
    m, d = x.shape
    d_ff = w_out.shape[0]
    nk = d_ff // tf
    assert m % tm == 0 and d_ff % tf == 0 and nk % blocks_per_step == 0 and nk >= 2
    assert tm % row_chunk == 0
    vmem = (2 * tm * d * 4
            + tm * d * 2
            + 2 * 3 * d * tf * w_in.dtype.itemsize)
    return pl.pallas_call(
        functools.partial(_ffn_kernel, final_norm=final_norm, blocks_per_step=blocks_per_step,
                          row_chunk=row_chunk),
        grid=(m // tm, nk // blocks_per_step),
        in_specs=[
            pl.BlockSpec(memory_space=pl.ANY),
            pl.BlockSpec((1, d), lambda i, k: (0, 0)),
            pl.BlockSpec(memory_space=pl.ANY),
            pl.BlockSpec(memory_space=pl.ANY),
            pl.BlockSpec((1, d), lambda i, k: (0, 0)),
        ],
        out_specs=pl.BlockSpec(memory_space=pl.ANY),
        out_shape=jax.ShapeDtypeStruct((m, d), F32),
        scratch_shapes=[pltpu.VMEM((2, tm, d), F32), pltpu.VMEM((tm, d), BF16),
                        pltpu.VMEM((2, d, tf), w_in.dtype), pltpu.VMEM((2, d, tf), w_in.dtype),
                        pltpu.VMEM((2, tf, d), w_out.dtype),
                        pltpu.SemaphoreType.DMA(()), pltpu.SemaphoreType.DMA(()),
                        pltpu.SemaphoreType.DMA((3, 2))],
        compiler_params=pltpu.CompilerParams(
            dimension_semantics=("arbitrary", "arbitrary"),
            vmem_limit_bytes=_vmem_limit(vmem)),
        name="ffn_final" if final_norm else "ffn",
    )(x, g, w_in, w_out, g_final)


def _stream_row_tiles(s, per, n_tiles, h_hbm, o_hbm, buf_ref, in_sem, out_sem):
    tm = buf_ref.shape[1]
    t = s // per
    j = lax.rem(s, per)

    def load(tile):
        return pltpu.make_async_copy(h_hbm.at[pl.ds(tile * tm, tm)],
                                     buf_ref.at[lax.rem(tile, 2)], in_sem)

    def store(tile):
        return pltpu.make_async_copy(buf_ref.at[lax.rem(tile, 2)],
                                     o_hbm.at[pl.ds(tile * tm, tm)], out_sem)

    @pl.when(s == 0)
    def _():
        load(0).start()

    @pl.when((j == 0) & (t < n_tiles))
    def _():
        load(t).wait()

    @pl.when((j == 1) & (t >= 1))
    def _():
        store(t - 1).start()

    @pl.when(j == 2)
    def _():
        @pl.when(t >= 1)
        def _():
            store(t - 1).wait()

        @pl.when(t + 1 < n_tiles)
        def _():
            load(t + 1).start()

    return store


def _mix_kernel(h_hbm, g_ref, wb_ref, wc_ref, wh_ref, wu_ref, wv_ref, cw_ref, cb_ref,
                gv_ref, ws_ref, bs_ref, woc_ref, wog_ref, o_hbm, buf_ref, hn_ref, y_ref, carry_ref,
                in_sem, out_sem, *, nj, n_tiles, tiles_per_seq):
    s = pl.program_id(0)
    j = lax.rem(s, nj)
    cur = lax.rem(s, 2)
    prev = 1 - cur
    tm = hn_ref.shape[0]
    tile_slot = lax.rem(s // nj, 2)
    prev_slot = lax.rem(jnp.maximum(s - 1, 0) // nj, 2)
    store = _stream_row_tiles(s, nj, n_tiles, h_hbm, o_hbm, buf_ref, in_sem, out_sem)

    @pl.when(s == 0)
    def _():
        y_ref[prev] = jnp.zeros(y_ref.shape[1:], BF16)

    @pl.when((j == 0) & (s < n_tiles * nj))
    def _():
        hn_ref[...] = _rmsnorm(buf_ref[tile_slot], g_ref[...]).astype(BF16)

    @pl.when(lax.rem(s // nj, tiles_per_seq) == 0)
    def _():
        carry_ref[j] = jnp.zeros((SUBLANES, MIX_COLS), F32)

    w_in = jnp.concatenate([r[...].astype(BF16) for r in (wv_ref, wu_ref, wc_ref, wh_ref, wb_ref)],
                           axis=1)
    proj = _dot(hn_ref[...], w_in)
    v, u, gate_c, h_c, gate_b = (proj[:, n * MIX_COLS:(n + 1) * MIX_COLS] for n in range(5))

    y_prev = y_ref[prev]
    buf_ref[prev_slot] += _dot(y_prev[:, :MIX_COLS], woc_ref[...].astype(BF16))

    gv = gv_ref[...]
    bs = bs_ref[...]
    tri = (lax.broadcasted_iota(jnp.int32, (CHUNK, CHUNK), 0)
           >= lax.broadcasted_iota(jnp.int32, (CHUNK, CHUNK), 1))
    sgs = []
    for hh in range(MIX_COLS // GROUP_DIM):
        lanes = slice(hh * GROUP_DIM, (hh + 1) * GROUP_DIM)
        vh = _rmsnorm(v[:, lanes], gv[:, lanes]).astype(BF16)
        w = jnp.where(tri, ws_ref[hh], 0.0).astype(BF16)
        sgs.append([_dot(w, vh[c * CHUNK:(c + 1) * CHUNK]) for c in range(tm // CHUNK)])

    buf_ref[prev_slot] += _dot(y_prev[:, MIX_COLS:], wog_ref[...].astype(BF16))

    for hh, per_chunk in enumerate(sgs):
        lanes = slice(hh * GROUP_DIM, (hh + 1) * GROUP_DIM)
        for c, sg in enumerate(per_chunk):
            rows = slice(c * CHUNK, (c + 1) * CHUNK)
            y_ref[cur, rows, MIX_COLS + hh * GROUP_DIM:MIX_COLS + (hh + 1) * GROUP_DIM] = (
                u[rows, lanes] * (sg + bs[:, hh:hh + 1])).astype(BF16)

    z = gate_c * h_c
    tail = carry_ref[j]
    carry_ref[j] = z[tm - SUBLANES:, :]
    cw = cw_ref[...]
    cb = cb_ref[...]
    row = lax.broadcasted_iota(jnp.int32, (SUBLANES, MIX_COLS), 0)
    z1 = pltpu.roll(z, 1, 0)
    z2 = pltpu.roll(z, 2, 0)
    head1 = jnp.where(row < 1, pltpu.roll(tail, 1, 0), z1[:SUBLANES])
    head2 = jnp.where(row < 2, pltpu.roll(tail, 2, 0), z2[:SUBLANES])
    z1 = jnp.concatenate([head1, z1[SUBLANES:]], axis=0)
    z2 = jnp.concatenate([head2, z2[SUBLANES:]], axis=0)
    y_ref[cur, :, :MIX_COLS] = (
        gate_b * (cb + cw[2:3] * z + cw[1:2] * z1 + cw[0:1] * z2)).astype(BF16)

    @pl.when(s == n_tiles * nj)
    def _():
        store(n_tiles - 1).start()
        store(n_tiles - 1).wait()


def _mix(h, g, w_in, conv_w, conv_b, g_v, w_s, b_s_t, w_out, *, seq, tm):
    m, d = h.shape
    width = w_out.shape[0]
    conv_width = width // 2
    nj = conv_width // MIX_COLS
    heads_per_step = MIX_COLS // GROUP_DIM
    n_tiles = m // tm
    n_steps = n_tiles * nj
    assert m % tm == 0 and seq % tm == 0 and tm % CHUNK == 0 and nj >= 3

    def prev_j(s):
        return jnp.maximum(s - 1, 0) % nj

    def col_block(offset):
        return pl.BlockSpec((d, MIX_COLS), lambda s: (0, offset + s % nj))

    vmem = (2 * tm * d * 4 + tm * d * 2 + 2 * tm * 2 * MIX_COLS * 2
            + 2 * 5 * d * MIX_COLS * w_in.dtype.itemsize
            + 2 * 2 * MIX_COLS * d * w_out.dtype.itemsize)
    return pl.pallas_call(
        functools.partial(_mix_kernel, nj=nj, n_tiles=n_tiles, tiles_per_seq=seq // tm),
        grid=(n_steps + 1,),
        in_specs=[
            pl.BlockSpec(memory_space=pl.ANY),
            pl.BlockSpec((1, d), lambda s: (0, 0)),
            col_block(0), col_block(nj), col_block(2 * nj),
            col_block(3 * nj), col_block(4 * nj),
            pl.BlockSpec((CONV_K, MIX_COLS), lambda s: (0, s % nj)),
            pl.BlockSpec((1, MIX_COLS), lambda s: (0, s % nj)),
            pl.BlockSpec((1, MIX_COLS), lambda s: (0, s % nj)),
            pl.BlockSpec((heads_per_step, CHUNK, CHUNK), lambda s: (s % nj, 0, 0)),
            pl.BlockSpec((None, CHUNK, heads_per_step), lambda s: (s % nj, 0, 0)),
            pl.BlockSpec((MIX_COLS, d), lambda s: (prev_j(s), 0)),
            pl.BlockSpec((MIX_COLS, d), lambda s: (nj + prev_j(s), 0)),
        ],
        out_specs=pl.BlockSpec(memory_space=pl.ANY),
        out_shape=jax.ShapeDtypeStruct((m, d), F32),
        scratch_shapes=[pltpu.VMEM((2, tm, d), F32), pltpu.VMEM((tm, d), BF16),
                        pltpu.VMEM((2, tm, 2 * MIX_COLS), BF16),
                        pltpu.VMEM((nj, SUBLANES, MIX_COLS), F32),
                        pltpu.SemaphoreType.DMA(()), pltpu.SemaphoreType.DMA(())],
        compiler_params=pltpu.CompilerParams(
            dimension_semantics=("arbitrary",), vmem_limit_bytes=_vmem_limit(vmem)),
        name="mix",
    )(h, g, w_in, w_in, w_in, w_in, w_in, conv_w, conv_b, g_v, w_s, b_s_t, w_out, w_out)


def _kv_kernel(mem_ref, g_ref, wk_ref, wv_ref, k_ref, v_ref, mn_ref):
    @pl.when(pl.program_id(0) == 0)
    def _():
        mn_ref[...] = _rmsnorm(mem_ref[...], g_ref[...]).astype(BF16)

    mn = mn_ref[...]
    k_ref[...] = _dot(mn, wk_ref[...].astype(BF16)).astype(BF16)
    v_ref[...] = _dot(mn, wv_ref[...].astype(BF16)).astype(BF16)


def _kv(mem, g, w_k, w_v, *, tn):
    m, d = mem.shape
    vmem = (2 * m * d * 4 + m * d * 2 + 2 * 2 * d * tn * w_k.dtype.itemsize
            + 2 * 2 * m * tn * 2)
    return pl.pallas_call(
        _kv_kernel,
        grid=(d // tn,),
        in_specs=[
            pl.BlockSpec((m, d), lambda n: (0, 0)),
            pl.BlockSpec((1, d), lambda n: (0, 0)),
            pl.BlockSpec((d, tn), lambda n: (0, n)),
            pl.BlockSpec((d, tn), lambda n: (0, n)),
        ],
        out_specs=[pl.BlockSpec((m, tn), lambda n: (0, n))] * 2,
        out_shape=[jax.ShapeDtypeStruct((m, d), BF16)] * 2,
        scratch_shapes=[pltpu.VMEM((m, d), BF16)],
        compiler_params=pltpu.CompilerParams(
            dimension_semantics=("arbitrary",), vmem_limit_bytes=_vmem_limit(vmem)),
        name="kv",
    )(mem, g, w_k, w_v)


def _xattn_kernel(h_hbm, g_ref, wq_ref, k_ref, v_ref, wo_ref, o_hbm, buf_ref, hn_ref, q_ref,
                  in_sem, out_sem, *, n_tiles):
    s = pl.program_id(0)
    cur = lax.rem(s, 2)
    prev = 1 - cur
    tile_slot = lax.rem(s // XA_HEADS, 2)
    prev_slot = lax.rem(jnp.maximum(s - 1, 0) // XA_HEADS, 2)
    store = _stream_row_tiles(s, XA_HEADS, n_tiles, h_hbm, o_hbm, buf_ref, in_sem, out_sem)

    @pl.when(s == 0)
    def _():
        q_ref[prev] = jnp.zeros(q_ref.shape[1:], BF16)

    @pl.when((lax.rem(s, XA_HEADS) == 0) & (s < n_tiles * XA_HEADS))
    def _():
        hn_ref[...] = _rmsnorm(buf_ref[tile_slot], g_ref[...]).astype(BF16)

    q_prev = q_ref[prev]
    scores = lax.dot_general(q_prev, k_ref[...], (((1,), (1,)), ((), ())),
                             preferred_element_type=F32)
    q_ref[cur] = _dot(hn_ref[...], wq_ref[...].astype(BF16)).astype(BF16)
    scores = scores * (q_prev.shape[-1] ** -0.5)
    p = jnp.exp(scores - jnp.max(scores, axis=-1, keepdims=True))
    p = p / jnp.sum(p, axis=-1, keepdims=True)
    o = _dot(p.astype(BF16), v_ref[...])
    o = jnp.where(s > 0, o, 0.0)
    buf_ref[prev_slot] += _dot(o.astype(BF16), wo_ref[...].astype(BF16))

    @pl.when(s == n_tiles * XA_HEADS)
    def _():
        store(n_tiles - 1).start()
        store(n_tiles - 1).wait()


def _xattn(h, g, w_q, k, v, w_o, *, seq, n_mem, tm):
    m, d = h.shape
    hdim = d // XA_HEADS
    tiles_per_seq = seq // tm
    n_tiles = m // tm
    n_steps = n_tiles * XA_HEADS
    assert m % tm == 0 and seq % tm == 0 and XA_HEADS >= 3

    def prev_tile(s):
        return jnp.maximum(s - 1, 0) // XA_HEADS

    def prev_head(s):
        return jnp.maximum(s - 1, 0) % XA_HEADS

    vmem = (2 * tm * d * 4 + tm * d * 2 + 2 * tm * hdim * 2
            + 2 * 2 * d * hdim * w_q.dtype.itemsize + 2 * 2 * n_mem * hdim * 2)
    return pl.pallas_call(
        functools.partial(_xattn_kernel, n_tiles=n_tiles),
        grid=(n_steps + 1,),
        in_specs=[
            pl.BlockSpec(memory_space=pl.ANY),
            pl.BlockSpec((1, d), lambda s: (0, 0)),
            pl.BlockSpec((d, hdim), lambda s: (0, s % XA_HEADS)),
            pl.BlockSpec((n_mem, hdim), lambda s: (prev_tile(s) // tiles_per_seq, prev_head(s))),
            pl.BlockSpec((n_mem, hdim), lambda s: (prev_tile(s) // tiles_per_seq, prev_head(s))),
            pl.BlockSpec((hdim, d), lambda s: (prev_head(s), 0)),
        ],
        out_specs=pl.BlockSpec(memory_space=pl.ANY),
        out_shape=jax.ShapeDtypeStruct((m, d), F32),
        scratch_shapes=[pltpu.VMEM((2, tm, d), F32), pltpu.VMEM((tm, d), BF16),
                        pltpu.VMEM((2, tm, hdim), BF16),
                        pltpu.SemaphoreType.DMA(()), pltpu.SemaphoreType.DMA(())],
        compiler_params=pltpu.CompilerParams(
            dimension_semantics=("arbitrary",), vmem_limit_bytes=_vmem_limit(vmem)),
        name="xattn",
    )(h, g, w_q, k, v, w_o)


def kernel(x, mem, g_ffn1, w_ffn1_in, w_ffn1_out, g_mix, w_mix_in, conv_w, conv_b, g_gm_v,
           w_spatial, b_spatial, w_mix_out, g_xattn, g_mem, w_xq, w_xk, w_xv, w_xo, g_ffn2,
           w_ffn2_in, w_ffn2_out, g_final):
    b, s, d = x.shape
    n_mem = mem.shape[1]
    depth = g_ffn1.shape[0]
    assert depth >= 1
    heads_per_step = MIX_COLS // GROUP_DIM
    ffn_tiles = dict(tm=2048, tf=256, blocks_per_step=11, row_chunk=1024)
    tm = 1024

    def row(v):
        return v.reshape(1, -1)

    h = x.reshape(b * s, d)
    mem2 = mem.reshape(b * n_mem, d)
    g_fin = row(g_final)
    for l in range(depth):
        last = l == depth - 1
        h = _ffn(h, row(g_ffn1[l]), w_ffn1_in[l], w_ffn1_out[l], g_fin,
                 final_norm=False, **ffn_tiles)
        b_s_t = b_spatial[l].reshape(-1, heads_per_step, CHUNK).transpose(0, 2, 1)
        h = _mix(h, row(g_mix[l]), w_mix_in[l], conv_w[l], row(conv_b[l]), row(g_gm_v[l]),
                 w_spatial[l], b_s_t, w_mix_out[l], seq=s, tm=tm)
        k, v = _kv(mem2, row(g_mem[l]), w_xk[l], w_xv[l], tn=512)
        h = _xattn(h, row(g_xattn[l]), w_xq[l], k, v, w_xo[l], seq=s, n_mem=n_mem, tm=tm)
        h = _ffn(h, row(g_ffn2[l]), w_ffn2_in[l], w_ffn2_out[l], g_fin,
                 final_norm=last, **ffn_tiles)
    return h.reshape(b, s, d)
```

```python
import functools

import jax
import jax.numpy as jnp
from jax import lax
from jax.experimental import pallas as pl
from jax.experimental.pallas import tpu as pltpu

GROUP_DIM = 128
CHUNK = 128
CONV_K = 3
XA_HEADS = 4
EPS = 1e-6

V7X_VMEM_BYTES = 64 * 1024 * 1024
SUBLANES = 8
MIX_COLS = 256

BF16 = jnp.bfloat16
F32 = jnp.float32


def _rmsnorm(x, g):
    y = x * lax.rsqrt(jnp.mean(x * x, axis=-1, keepdims=True) + EPS)
    return y * g


_dot = functools.partial(jnp.dot, preferred_element_type=F32)


def _vmem_limit(nbytes):
    return min(int(nbytes * 1.25) + (8 << 20), V7X_VMEM_BYTES - (4 << 20))


def _ffn_kernel(x_hbm, g_ref, win_hbm, wout_hbm, gf_ref, o_hbm, buf_ref, xn_ref, wg_ref, wu_ref,
                wo_ref, in_sem, out_sem, w_sem, *, final_norm, row_chunk, unroll):
    i = pl.program_id(0)
    n_tiles = pl.num_programs(0)
    tm, d = xn_ref.shape
    tf = wo_ref.shape[1]
    d_ff = wout_hbm.shape[0]
    nk = d_ff // tf
    slot = lax.rem(i, 2)
    other = 1 - slot

    def load(tile, dst):
        return pltpu.make_async_copy(x_hbm.at[pl.ds(tile * tm, tm)], buf_ref.at[dst], in_sem)

    def store(tile, src):
        return pltpu.make_async_copy(buf_ref.at[src], o_hbm.at[pl.ds(tile * tm, tm)], out_sem)

    def weight_copies(k, ws):
        cols = pl.ds(pl.multiple_of(k * tf, tf), tf)
        up_cols = pl.ds(pl.multiple_of(d_ff + k * tf, tf), tf)
        return (pltpu.make_async_copy(win_hbm.at[:, cols], wg_ref.at[ws], w_sem.at[0, ws]),
                pltpu.make_async_copy(win_hbm.at[:, up_cols], wu_ref.at[ws], w_sem.at[1, ws]),
                pltpu.make_async_copy(wout_hbm.at[cols, :], wo_ref.at[ws], w_sem.at[2, ws]))

    @pl.when(i == 0)
    def _():
        load(0, 0).start()
        for c in weight_copies(0, 0):
            c.start()

    load(i, slot).wait()
    xn_ref[...] = _rmsnorm(buf_ref[slot], g_ref[...]).astype(BF16)

    @pl.when(i > 0)
    def _():
        store(i - 1, other).start()

    def block(k, carry):
        @pl.when(k == nk // 2)
        def _():
            @pl.when(i > 0)
            def _():
                store(i - 1, other).wait()

            @pl.when(i + 1 < n_tiles)
            def _():
                load(i + 1, other).start()

        ws = lax.rem(i * nk + k, 2)
        for c in weight_copies(k, ws):
            c.wait()

        for c in weight_copies(lax.rem(k + 1, nk), 1 - ws):
            c.start()

        w_gate, w_up, w_down = (r[ws].astype(BF16) for r in (wg_ref, wu_ref, wo_ref))
        for rows in (pl.ds(r * row_chunk, row_chunk) for r in range(tm // row_chunk)):
            xn = xn_ref[rows, :]
            gate = _dot(xn, w_gate)
            up = _dot(xn, w_up)
            hidden = (0.5 * gate * jax.nn.sigmoid(gate) * up).astype(BF16)
            buf_ref[slot, rows, :] += _dot(hidden, w_down)
        return carry

    lax.fori_loop(0, nk, block, 0, unroll=unroll)

    if final_norm:
        buf_ref[slot] = _rmsnorm(buf_ref[slot], gf_ref[...])

    @pl.when(i == n_tiles - 1)
    def _():
        store(i, slot).start()
        for c in weight_copies(0, lax.rem(n_tiles * nk, 2)):
            c.wait()
        store(i, slot).wait()


def _ffn(x, g, w_in, w_out, g_final, *, final_norm, tm, tf, row_chunk, unroll):
    m, d = x.shape
    d_ff = w_out.shape[0]
    nk = d_ff // tf
    assert m % tm == 0 and d_ff % tf == 0 and nk >= 2 and tm % row_chunk == 0
    vmem = (2 * tm * d * 4
            + tm * d * 2
            + 2 * 3 * d * tf * w_in.dtype.itemsize)
    return pl.pallas_call(
        functools.partial(_ffn_kernel, final_norm=final_norm, row_chunk=row_chunk, unroll=unroll),
        grid=(m // tm,),
        in_specs=[
            pl.BlockSpec(memory_space=pl.ANY),
            pl.BlockSpec((1, d), lambda i: (0, 0)),
            pl.BlockSpec(memory_space=pl.ANY),
            pl.BlockSpec(memory_space=pl.ANY),
            pl.BlockSpec((1, d), lambda i: (0, 0)),
        ],
        out_specs=pl.BlockSpec(memory_space=pl.ANY),
        out_shape=jax.ShapeDtypeStruct((m, d), F32),
        scratch_shapes=[pltpu.VMEM((2, tm, d), F32), pltpu.VMEM((tm, d), BF16),
                        pltpu.VMEM((2, d, tf), w_in.dtype), pltpu.VMEM((2, d, tf), w_in.dtype),
                        pltpu.VMEM((2, tf, d), w_out.dtype),
                        pltpu.SemaphoreType.DMA(()), pltpu.SemaphoreType.DMA(()),
                        pltpu.SemaphoreType.DMA((3, 2))],
        compiler_params=pltpu.CompilerParams(
            dimension_semantics=("arbitrary",), vmem_limit_bytes=_vmem_limit(vmem)),
        name="ffn_final" if final_norm else "ffn",
    )(x, g, w_in, w_out, g_final)


def _stream_row_tiles(s, per, n_tiles, h_hbm, o_hbm, buf_ref, in_sem, out_sem):
    tm = buf_ref.shape[1]
    t = s // per
    j = lax.rem(s, per)

    def load(tile):
        return pltpu.make_async_copy(h_hbm.at[pl.ds(tile * tm, tm)],
                                     buf_ref.at[lax.rem(tile, 2)], in_sem)

    def store(tile):
        return pltpu.make_async_copy(buf_ref.at[lax.rem(tile, 2)],
                                     o_hbm.at[pl.ds(tile * tm, tm)], out_sem)

    @pl.when(s == 0)
    def _():
        load(0).start()

    @pl.when((j == 0) & (t < n_tiles))
    def _():
        load(t).wait()

    @pl.when((j == 1) & (t >= 1))
    def _():
        store(t - 1).start()

    @pl.when(j == 2)
    def _():
        @pl.when(t >= 1)
        def _():
            store(t - 1).wait()

        @pl.when(t + 1 < n_tiles)
        def _():
            load(t + 1).start()

    return store


def _mix_kernel(h_hbm, g_ref, wb_ref, wc_ref, wh_ref, wu_ref, wv_ref, cw_ref, cb_ref,
                gv_ref, ws_ref, bs_ref, woc_ref, wog_ref, o_hbm, buf_ref, hn_ref, y_ref, carry_ref,
                in_sem, out_sem, *, nj, n_tiles, tiles_per_seq):
    s = pl.program_id(0)
    j = lax.rem(s, nj)
    cur = lax.rem(s, 2)
    prev = 1 - cur
    tm = hn_ref.shape[0]
    tile_slot = lax.rem(s // nj, 2)
    prev_slot = lax.rem(jnp.maximum(s - 1, 0) // nj, 2)
    store = _stream_row_tiles(s, nj, n_tiles, h_hbm, o_hbm, buf_ref, in_sem, out_sem)

    @pl.when(s == 0)
    def _():
        y_ref[prev] = jnp.zeros(y_ref.shape[1:], BF16)

    @pl.when((j == 0) & (s < n_tiles * nj))
    def _():
        hn_ref[...] = _rmsnorm(buf_ref[tile_slot], g_ref[...]).astype(BF16)

    @pl.when(lax.rem(s // nj, tiles_per_seq) == 0)
    def _():
        carry_ref[j] = jnp.zeros((SUBLANES, MIX_COLS), F32)

    w_in = jnp.concatenate([r[...].astype(BF16) for r in (wv_ref, wu_ref, wc_ref, wh_ref, wb_ref)],
                           axis=1)
    proj = _dot(hn_ref[...], w_in)
    v, u, gate_c, h_c, gate_b = (proj[:, n * MIX_COLS:(n + 1) * MIX_COLS] for n in range(5))

    y_prev = y_ref[prev]
    buf_ref[prev_slot] += _dot(y_prev[:, :MIX_COLS], woc_ref[...].astype(BF16))

    gv = gv_ref[...]
    bs = bs_ref[...]
    tri = (lax.broadcasted_iota(jnp.int32, (CHUNK, CHUNK), 0)
           >= lax.broadcasted_iota(jnp.int32, (CHUNK, CHUNK), 1))
    sgs = []
    for hh in range(MIX_COLS // GROUP_DIM):
        lanes = slice(hh * GROUP_DIM, (hh + 1) * GROUP_DIM)
        vh = _rmsnorm(v[:, lanes], gv[:, lanes]).astype(BF16)
        w = jnp.where(tri, ws_ref[hh], 0.0).astype(BF16)
        sgs.append([_dot(w, vh[c * CHUNK:(c + 1) * CHUNK]) for c in range(tm // CHUNK)])

    buf_ref[prev_slot] += _dot(y_prev[:, MIX_COLS:], wog_ref[...].astype(BF16))

    for hh, per_chunk in enumerate(sgs):
        lanes = slice(hh * GROUP_DIM, (hh + 1) * GROUP_DIM)
        for c, sg in enumerate(per_chunk):
            rows = slice(c * CHUNK, (c + 1) * CHUNK)
            y_ref[cur, rows, MIX_COLS + hh * GROUP_DIM:MIX_COLS + (hh + 1) * GROUP_DIM] = (
                u[rows, lanes] * (sg + bs[:, hh:hh + 1])).astype(BF16)

    z = gate_c * h_c
    tail = carry_ref[j]
    carry_ref[j] = z[tm - SUBLANES:, :]
    cw = cw_ref[...]
    cb = cb_ref[...]
    row = lax.broadcasted_iota(jnp.int32, (SUBLANES, MIX_COLS), 0)
    z1 = pltpu.roll(z, 1, 0)
    z2 = pltpu.roll(z, 2, 0)
    head1 = jnp.where(row < 1, pltpu.roll(tail, 1, 0), z1[:SUBLANES])
    head2 = jnp.where(row < 2, pltpu.roll(tail, 2, 0), z2[:SUBLANES])
    z1 = jnp.concatenate([head1, z1[SUBLANES:]], axis=0)
    z2 = jnp.concatenate([head2, z2[SUBLANES:]], axis=0)
    y_ref[cur, :, :MIX_COLS] = (
        gate_b * (cb + cw[2:3] * z + cw[1:2] * z1 + cw[0:1] * z2)).astype(BF16)

    @pl.when(s == n_tiles * nj)
    def _():
        store(n_tiles - 1).start()
        store(n_tiles - 1).wait()


def _mix(h, g, w_in, conv_w, conv_b, g_v, w_s, b_s_t, w_out, *, seq, tm):
    m, d = h.shape
    width = w_out.shape[0]
    conv_width = width // 2
    nj = conv_width // MIX_COLS
    heads_per_step = MIX_COLS // GROUP_DIM
    n_tiles = m // tm
    n_steps = n_tiles * nj
    assert m % tm == 0 and seq % tm == 0 and tm % CHUNK == 0 and nj >= 3

    def prev_j(s):
        return jnp.maximum(s - 1, 0) % nj

    def col_block(offset):
        return pl.BlockSpec((d, MIX_COLS), lambda s: (0, offset + s % nj))

    vmem = (2 * tm * d * 4 + tm * d * 2 + 2 * tm * 2 * MIX_COLS * 2
            + 2 * 5 * d * MIX_COLS * w_in.dtype.itemsize
            + 2 * 2 * MIX_COLS * d * w_out.dtype.itemsize)
    return pl.pallas_call(
        functools.partial(_mix_kernel, nj=nj, n_tiles=n_tiles, tiles_per_seq=seq // tm),
        grid=(n_steps + 1,),
        in_specs=[
            pl.BlockSpec(memory_space=pl.ANY),
            pl.BlockSpec((1, d), lambda s: (0, 0)),
            col_block(0), col_block(nj), col_block(2 * nj),
            col_block(3 * nj), col_block(4 * nj),
            pl.BlockSpec((CONV_K, MIX_COLS), lambda s: (0, s % nj)),
            pl.BlockSpec((1, MIX_COLS), lambda s: (0, s % nj)),
            pl.BlockSpec((1, MIX_COLS), lambda s: (0, s % nj)),
            pl.BlockSpec((heads_per_step, CHUNK, CHUNK), lambda s: (s % nj, 0, 0)),
            pl.BlockSpec((None, CHUNK, heads_per_step), lambda s: (s % nj, 0, 0)),
            pl.BlockSpec((MIX_COLS, d), lambda s: (prev_j(s), 0)),
            pl.BlockSpec((MIX_COLS, d), lambda s: (nj + prev_j(s), 0)),
        ],
        out_specs=pl.BlockSpec(memory_space=pl.ANY),
        out_shape=jax.ShapeDtypeStruct((m, d), F32),
        scratch_shapes=[pltpu.VMEM((2, tm, d), F32), pltpu.VMEM((tm, d), BF16),
                        pltpu.VMEM((2, tm, 2 * MIX_COLS), BF16),
                        pltpu.VMEM((nj, SUBLANES, MIX_COLS), F32),
                        pltpu.SemaphoreType.DMA(()), pltpu.SemaphoreType.DMA(())],
        compiler_params=pltpu.CompilerParams(
            dimension_semantics=("arbitrary",), vmem_limit_bytes=_vmem_limit(vmem)),
        name="mix",
    )(h, g, w_in, w_in, w_in, w_in, w_in, conv_w, conv_b, g_v, w_s, b_s_t, w_out, w_out)


def _kv_kernel(mem_ref, g_ref, wk_ref, wv_ref, k_ref, v_ref, mn_ref):
    @pl.when(pl.program_id(0) == 0)
    def _():
        mn_ref[...] = _rmsnorm(mem_ref[...], g_ref[...]).astype(BF16)

    mn = mn_ref[...]
    k_ref[...] = _dot(mn, wk_ref[...].astype(BF16)).astype(BF16)
    v_ref[...] = _dot(mn, wv_ref[...].astype(BF16)).astype(BF16)


def _kv(mem, g, w_k, w_v, *, tn):
    m, d = mem.shape
    vmem = (2 * m * d * 4 + m * d * 2 + 2 * 2 * d * tn * w_k.dtype.itemsize
            + 2 * 2 * m * tn * 2)
    return pl.pallas_call(
        _kv_kernel,
        grid=(d // tn,),
        in_specs=[
            pl.BlockSpec((m, d), lambda n: (0, 0)),
            pl.BlockSpec((1, d), lambda n: (0, 0)),
            pl.BlockSpec((d, tn), lambda n: (0, n)),
            pl.BlockSpec((d, tn), lambda n: (0, n)),
        ],
        out_specs=[pl.BlockSpec((m, tn), lambda n: (0, n))] * 2,
        out_shape=[jax.ShapeDtypeStruct((m, d), BF16)] * 2,
        scratch_shapes=[pltpu.VMEM((m, d), BF16)],
        compiler_params=pltpu.CompilerParams(
            dimension_semantics=("arbitrary",), vmem_limit_bytes=_vmem_limit(vmem)),
        name="kv",
    )(mem, g, w_k, w_v)


def _xattn_kernel(h_hbm, g_ref, wq_ref, k_ref, v_ref, wo_ref, o_hbm, buf_ref, hn_ref, q_ref,
                  in_sem, out_sem, *, n_tiles):
    s = pl.program_id(0)
    cur = lax.rem(s, 2)
    prev = 1 - cur
    tile_slot = lax.rem(s // XA_HEADS, 2)
    prev_slot = lax.rem(jnp.maximum(s - 1, 0) // XA_HEADS, 2)
    store = _stream_row_tiles(s, XA_HEADS, n_tiles, h_hbm, o_hbm, buf_ref, in_sem, out_sem)

    @pl.when(s == 0)
    def _():
        q_ref[prev] = jnp.zeros(q_ref.shape[1:], BF16)

    @pl.when((lax.rem(s, XA_HEADS) == 0) & (s < n_tiles * XA_HEADS))
    def _():
        hn_ref[...] = _rmsnorm(buf_ref[tile_slot], g_ref[...]).astype(BF16)

    q_prev = q_ref[prev]
    scores = lax.dot_general(q_prev, k_ref[...], (((1,), (1,)), ((), ())),
                             preferred_element_type=F32)
    q_ref[cur] = _dot(hn_ref[...], wq_ref[...].astype(BF16)).astype(BF16)
    scores = scores * (q_prev.shape[-1] ** -0.5)
    p = jnp.exp(scores - jnp.max(scores, axis=-1, keepdims=True))
    p = p / jnp.sum(p, axis=-1, keepdims=True)
    o = _dot(p.astype(BF16), v_ref[...])
    o = jnp.where(s > 0, o, 0.0)
    buf_ref[prev_slot] += _dot(o.astype(BF16), wo_ref[...].astype(BF16))

    @pl.when(s == n_tiles * XA_HEADS)
    def _():
        store(n_tiles - 1).start()
        store(n_tiles - 1).wait()


def _xattn(h, g, w_q, k, v, w_o, *, seq, n_mem, tm):
    m, d = h.shape
    hdim = d // XA_HEADS
    tiles_per_seq = seq // tm
    n_tiles = m // tm
    n_steps = n_tiles * XA_HEADS
    assert m % tm == 0 and seq % tm == 0 and XA_HEADS >= 3

    def prev_tile(s):
        return jnp.maximum(s - 1, 0) // XA_HEADS

    def prev_head(s):
        return jnp.maximum(s - 1, 0) % XA_HEADS

    vmem = (2 * tm * d * 4 + tm * d * 2 + 2 * tm * hdim * 2
            + 2 * 2 * d * hdim * w_q.dtype.itemsize + 2 * 2 * n_mem * hdim * 2)
    return pl.pallas_call(
        functools.partial(_xattn_kernel, n_tiles=n_tiles),
        grid=(n_steps + 1,),
        in_specs=[
            pl.BlockSpec(memory_space=pl.ANY),
            pl.BlockSpec((1, d), lambda s: (0, 0)),
            pl.BlockSpec((d, hdim), lambda s: (0, s % XA_HEADS)),
            pl.BlockSpec((n_mem, hdim), lambda s: (prev_tile(s) // tiles_per_seq, prev_head(s))),
            pl.BlockSpec((n_mem, hdim), lambda s: (prev_tile(s) // tiles_per_seq, prev_head(s))),
            pl.BlockSpec((hdim, d), lambda s: (prev_head(s), 0)),
        ],
        out_specs=pl.BlockSpec(memory_space=pl.ANY),
        out_shape=jax.ShapeDtypeStruct((m, d), F32),
        scratch_shapes=[pltpu.VMEM((2, tm, d), F32), pltpu.VMEM((tm, d), BF16),
                        pltpu.VMEM((2, tm, hdim), BF16),
                        pltpu.SemaphoreType.DMA(()), pltpu.SemaphoreType.DMA(())],
        compiler_params=pltpu.CompilerParams(
            dimension_semantics=("arbitrary",), vmem_limit_bytes=_vmem_limit(vmem)),
        name="xattn",
    )(h, g, w_q, k, v, w_o)


def kernel(x, mem, g_ffn1, w_ffn1_in, w_ffn1_out, g_mix, w_mix_in, conv_w, conv_b, g_gm_v,
           w_spatial, b_spatial, w_mix_out, g_xattn, g_mem, w_xq, w_xk, w_xv, w_xo, g_ffn2,
           w_ffn2_in, w_ffn2_out, g_final):
    b, s, d = x.shape
    n_mem = mem.shape[1]
    depth = g_ffn1.shape[0]
    assert depth >= 1
    heads_per_step = MIX_COLS // GROUP_DIM
    ffn_tiles = dict(tm=2048, tf=256, row_chunk=1024, unroll=1)
    tm = 1024

    def row(v):
        return v.reshape(1, -1)

    h = x.reshape(b * s, d)
    mem2 = mem.reshape(b * n_mem, d)
    g_fin = row(g_final)
    for l in range(depth):
        last = l == depth - 1
        h = _ffn(h, row(g_ffn1[l]), w_ffn1_in[l], w_ffn1_out[l], g_fin,
                 final_norm=False, **ffn_tiles)
        b_s_t = b_spatial[l].reshape(-1, heads_per_step, CHUNK).transpose(0, 2, 1)
        h = _mix(h, row(g_mix[l]), w_mix_in[l], conv_w[l], row(conv_b[l]), row(g_gm_v[l]),
                 w_spatial[l], b_s_t, w_mix_out[l], seq=s, tm=tm)
        k, v = _kv(mem2, row(g_mem[l]), w_xk[l], w_xv[l], tn=512)
        h = _xattn(h, row(g_xattn[l]), w_xq[l], k, v, w_xo[l], seq=s, n_mem=n_mem, tm=tm)
        h = _ffn(h, row(g_ffn2[l]), w_ffn2_in[l], w_ffn2_out[l], g_fin,
                 final_norm=last, **ffn_tiles)
    return h.reshape(b, s, d)
```

```python
import functools

import jax
import jax.numpy as jnp
from jax import lax
from jax.experimental import pallas as pl
from jax.experimental.pallas import tpu as pltpu

GROUP_DIM = 128
CHUNK = 128
CONV_K = 3
XA_HEADS = 4
EPS = 1e-6

V7X_VMEM_BYTES = 64 * 1024 * 1024
SUBLANES = 8
MIX_COLS = 256
ROW_TILE_DMA_THREAD = 1

BF16 = jnp.bfloat16
F32 = jnp.float32


def _rmsnorm(x, g):
    y = x * lax.rsqrt(jnp.mean(x * x, axis=-1, keepdims=True) + EPS)
    return y * g


_dot = functools.partial(jnp.dot, preferred_element_type=F32)


def _vmem_limit(nbytes):
    return min(int(nbytes * 1.25) + (8 << 20), V7X_VMEM_BYTES - (4 << 20))


def _ffn_kernel(x_hbm, g_ref, win_hbm, wout_hbm, gf_ref, o_hbm, buf_ref, xn_ref, wg_ref, wu_ref,
                wo_ref, in_sem, out_sem, w_sem, *, final_norm, row_chunk, unroll):
    i = pl.program_id(0)
    n_tiles = pl.num_programs(0)
    tm, d = xn_ref.shape
    tf = wo_ref.shape[1]
    d_ff = wout_hbm.shape[0]
    nk = d_ff // tf
    slot = lax.rem(i, 2)
    other = 1 - slot

    def load(tile, dst):
        return pltpu.make_async_copy(x_hbm.at[pl.ds(tile * tm, tm)], buf_ref.at[dst], in_sem)

    def store(tile, src):
        return pltpu.make_async_copy(buf_ref.at[src], o_hbm.at[pl.ds(tile * tm, tm)], out_sem)

    def weight_copies(k, ws):
        cols = pl.ds(pl.multiple_of(k * tf, tf), tf)
        up_cols = pl.ds(pl.multiple_of(d_ff + k * tf, tf), tf)
        return (pltpu.make_async_copy(win_hbm.at[:, cols], wg_ref.at[ws], w_sem.at[0, ws]),
                pltpu.make_async_copy(win_hbm.at[:, up_cols], wu_ref.at[ws], w_sem.at[1, ws]),
                pltpu.make_async_copy(wout_hbm.at[cols, :], wo_ref.at[ws], w_sem.at[2, ws]))

    @pl.when(i == 0)
    def _():
        load(0, 0).start(priority=ROW_TILE_DMA_THREAD)
        for n, c in enumerate(weight_copies(0, 0)):
            c.start(priority=n % 2)

    load(i, slot).wait()
    xn_ref[...] = _rmsnorm(buf_ref[slot], g_ref[...]).astype(BF16)

    @pl.when(i > 0)
    def _():
        store(i - 1, other).start(priority=ROW_TILE_DMA_THREAD)

    def block(k, carry):
        @pl.when(k == nk // 2)
        def _():
            @pl.when(i > 0)
            def _():
                store(i - 1, other).wait()

            @pl.when(i + 1 < n_tiles)
            def _():
                load(i + 1, other).start(priority=ROW_TILE_DMA_THREAD)

        ws = lax.rem(i * nk + k, 2)
        for c in weight_copies(k, ws):
            c.wait()

        for n, c in enumerate(weight_copies(lax.rem(k + 1, nk), 1 - ws)):
            c.start(priority=n % 2)

        w_gate, w_up, w_down = (r[ws].astype(BF16) for r in (wg_ref, wu_ref, wo_ref))
        for rows in (pl.ds(r * row_chunk, row_chunk) for r in range(tm // row_chunk)):
            xn = xn_ref[rows, :]
            gate = _dot(xn, w_gate)
            up = _dot(xn, w_up)
            hidden = (0.5 * gate * jax.nn.sigmoid(gate) * up).astype(BF16)
            buf_ref[slot, rows, :] += _dot(hidden, w_down)
        return carry

    lax.fori_loop(0, nk, block, 0, unroll=unroll)

    if final_norm:
        buf_ref[slot] = _rmsnorm(buf_ref[slot], gf_ref[...])

    @pl.when(i == n_tiles - 1)
    def _():
        store(i, slot).start(priority=ROW_TILE_DMA_THREAD)
        for c in weight_copies(0, lax.rem(n_tiles * nk, 2)):
            c.wait()
        store(i, slot).wait()


def _ffn(x, g, w_in, w_out, g_final, *, final_norm, tm, tf, row_chunk, unroll):
    m, d = x.shape
    d_ff = w_out.shape[0]
    nk = d_ff // tf
    assert m % tm == 0 and d_ff % tf == 0 and nk >= 2 and tm % row_chunk == 0
    vmem = (2 * tm * d * 4
            + tm * d * 2
            + 2 * 3 * d * tf * w_in.dtype.itemsize)
    return pl.pallas_call(
        functools.partial(_ffn_kernel, final_norm=final_norm, row_chunk=row_chunk, unroll=unroll),
        grid=(m // tm,),
        in_specs=[
            pl.BlockSpec(memory_space=pl.ANY),
            pl.BlockSpec((1, d), lambda i: (0, 0)),
            pl.BlockSpec(memory_space=pl.ANY),
            pl.BlockSpec(memory_space=pl.ANY),
            pl.BlockSpec((1, d), lambda i: (0, 0)),
        ],
        out_specs=pl.BlockSpec(memory_space=pl.ANY),
        out_shape=jax.ShapeDtypeStruct((m, d), F32),
        scratch_shapes=[pltpu.VMEM((2, tm, d), F32), pltpu.VMEM((tm, d), BF16),
                        pltpu.VMEM((2, d, tf), w_in.dtype), pltpu.VMEM((2, d, tf), w_in.dtype),
                        pltpu.VMEM((2, tf, d), w_out.dtype),
                        pltpu.SemaphoreType.DMA(()), pltpu.SemaphoreType.DMA(()),
                        pltpu.SemaphoreType.DMA((3, 2))],
        compiler_params=pltpu.CompilerParams(
            dimension_semantics=("arbitrary",), vmem_limit_bytes=_vmem_limit(vmem)),
        name="ffn_final" if final_norm else "ffn",
    )(x, g, w_in, w_out, g_final)


def _stream_row_tiles(s, per, n_tiles, h_hbm, o_hbm, buf_ref, in_sem, out_sem):
    tm = buf_ref.shape[1]
    t = s // per
    j = lax.rem(s, per)

    def load(tile):
        return pltpu.make_async_copy(h_hbm.at[pl.ds(tile * tm, tm)],
                                     buf_ref.at[lax.rem(tile, 2)], in_sem)

    def store(tile):
        return pltpu.make_async_copy(buf_ref.at[lax.rem(tile, 2)],
                                     o_hbm.at[pl.ds(tile * tm, tm)], out_sem)

    @pl.when(s == 0)
    def _():
        load(0).start(priority=ROW_TILE_DMA_THREAD)

    @pl.when((j == 0) & (t < n_tiles))
    def _():
        load(t).wait()

    @pl.when((j == 1) & (t >= 1))
    def _():
        store(t - 1).start(priority=ROW_TILE_DMA_THREAD)

    @pl.when(j == 2)
    def _():
        @pl.when(t >= 1)
        def _():
            store(t - 1).wait()

        @pl.when(t + 1 < n_tiles)
        def _():
            load(t + 1).start(priority=ROW_TILE_DMA_THREAD)

    return store


def _mix_kernel(h_hbm, g_ref, wb_ref, wc_ref, wh_ref, wu_ref, wv_ref, cw_ref, cb_ref,
                gv_ref, ws_ref, bs_ref, woc_ref, wog_ref, o_hbm, buf_ref, hn_ref, y_ref, carry_ref,
                in_sem, out_sem, *, nj, n_tiles, tiles_per_seq):
    s = pl.program_id(0)
    j = lax.rem(s, nj)
    cur = lax.rem(s, 2)
    prev = 1 - cur
    tm = hn_ref.shape[0]
    tile_slot = lax.rem(s // nj, 2)
    prev_slot = lax.rem(jnp.maximum(s - 1, 0) // nj, 2)
    store = _stream_row_tiles(s, nj, n_tiles, h_hbm, o_hbm, buf_ref, in_sem, out_sem)

    @pl.when(s == 0)
    def _():
        y_ref[prev] = jnp.zeros(y_ref.shape[1:], BF16)

    @pl.when((j == 0) & (s < n_tiles * nj))
    def _():
        hn_ref[...] = _rmsnorm(buf_ref[tile_slot], g_ref[...]).astype(BF16)

    @pl.when(lax.rem(s // nj, tiles_per_seq) == 0)
    def _():
        carry_ref[j] = jnp.zeros((SUBLANES, MIX_COLS), F32)

    w_in = jnp.concatenate([r[...].astype(BF16) for r in (wv_ref, wu_ref, wc_ref, wh_ref, wb_ref)],
                           axis=1)
    proj = _dot(hn_ref[...], w_in)
    v, u, gate_c, h_c, gate_b = (proj[:, n * MIX_COLS:(n + 1) * MIX_COLS] for n in range(5))

    y_prev = y_ref[prev]
    buf_ref[prev_slot] += _dot(y_prev[:, :MIX_COLS], woc_ref[...].astype(BF16))

    gv = gv_ref[...]
    bs = bs_ref[...]
    tri = (lax.broadcasted_iota(jnp.int32, (CHUNK, CHUNK), 0)
           >= lax.broadcasted_iota(jnp.int32, (CHUNK, CHUNK), 1))
    sgs = []
    for hh in range(MIX_COLS // GROUP_DIM):
        lanes = slice(hh * GROUP_DIM, (hh + 1) * GROUP_DIM)
        vh = _rmsnorm(v[:, lanes], gv[:, lanes]).astype(BF16)
        w = jnp.where(tri, ws_ref[hh], 0.0).astype(BF16)
        sgs.append([_dot(w, vh[c * CHUNK:(c + 1) * CHUNK]) for c in range(tm // CHUNK)])

    buf_ref[prev_slot] += _dot(y_prev[:, MIX_COLS:], wog_ref[...].astype(BF16))

    for hh, per_chunk in enumerate(sgs):
        lanes = slice(hh * GROUP_DIM, (hh + 1) * GROUP_DIM)
        for c, sg in enumerate(per_chunk):
            rows = slice(c * CHUNK, (c + 1) * CHUNK)
            y_ref[cur, rows, MIX_COLS + hh * GROUP_DIM:MIX_COLS + (hh + 1) * GROUP_DIM] = (
                u[rows, lanes] * (sg + bs[:, hh:hh + 1])).astype(BF16)

    z = gate_c * h_c
    tail = carry_ref[j]
    carry_ref[j] = z[tm - SUBLANES:, :]
    cw = cw_ref[...]
    cb = cb_ref[...]
    row = lax.broadcasted_iota(jnp.int32, (SUBLANES, MIX_COLS), 0)
    z1 = pltpu.roll(z, 1, 0)
    z2 = pltpu.roll(z, 2, 0)
    head1 = jnp.where(row < 1, pltpu.roll(tail, 1, 0), z1[:SUBLANES])
    head2 = jnp.where(row < 2, pltpu.roll(tail, 2, 0), z2[:SUBLANES])
    z1 = jnp.concatenate([head1, z1[SUBLANES:]], axis=0)
    z2 = jnp.concatenate([head2, z2[SUBLANES:]], axis=0)
    y_ref[cur, :, :MIX_COLS] = (
        gate_b * (cb + cw[2:3] * z + cw[1:2] * z1 + cw[0:1] * z2)).astype(BF16)

    @pl.when(s == n_tiles * nj)
    def _():
        store(n_tiles - 1).start(priority=ROW_TILE_DMA_THREAD)
        store(n_tiles - 1).wait()


def _mix(h, g, w_in, conv_w, conv_b, g_v, w_s, b_s_t, w_out, *, seq, tm):
    m, d = h.shape
    width = w_out.shape[0]
    conv_width = width // 2
    nj = conv_width // MIX_COLS
    heads_per_step = MIX_COLS // GROUP_DIM
    n_tiles = m // tm
    n_steps = n_tiles * nj
    assert m % tm == 0 and seq % tm == 0 and tm % CHUNK == 0 and nj >= 3

    def prev_j(s):
        return jnp.maximum(s - 1, 0) % nj

    def col_block(offset):
        return pl.BlockSpec((d, MIX_COLS), lambda s: (0, offset + s % nj))

    vmem = (2 * tm * d * 4 + tm * d * 2 + 2 * tm * 2 * MIX_COLS * 2
            + 2 * 5 * d * MIX_COLS * w_in.dtype.itemsize
            + 2 * 2 * MIX_COLS * d * w_out.dtype.itemsize)
    return pl.pallas_call(
        functools.partial(_mix_kernel, nj=nj, n_tiles=n_tiles, tiles_per_seq=seq // tm),
        grid=(n_steps + 1,),
        in_specs=[
            pl.BlockSpec(memory_space=pl.ANY),
            pl.BlockSpec((1, d), lambda s: (0, 0)),
            col_block(0), col_block(nj), col_block(2 * nj),
            col_block(3 * nj), col_block(4 * nj),
            pl.BlockSpec((CONV_K, MIX_COLS), lambda s: (0, s % nj)),
            pl.BlockSpec((1, MIX_COLS), lambda s: (0, s % nj)),
            pl.BlockSpec((1, MIX_COLS), lambda s: (0, s % nj)),
            pl.BlockSpec((heads_per_step, CHUNK, CHUNK), lambda s: (s % nj, 0, 0)),
            pl.BlockSpec((None, CHUNK, heads_per_step), lambda s: (s % nj, 0, 0)),
            pl.BlockSpec((MIX_COLS, d), lambda s: (prev_j(s), 0)),
            pl.BlockSpec((MIX_COLS, d), lambda s: (nj + prev_j(s), 0)),
        ],
        out_specs=pl.BlockSpec(memory_space=pl.ANY),
        out_shape=jax.ShapeDtypeStruct((m, d), F32),
        scratch_shapes=[pltpu.VMEM((2, tm, d), F32), pltpu.VMEM((tm, d), BF16),
                        pltpu.VMEM((2, tm, 2 * MIX_COLS), BF16),
                        pltpu.VMEM((nj, SUBLANES, MIX_COLS), F32),
                        pltpu.SemaphoreType.DMA(()), pltpu.SemaphoreType.DMA(())],
        compiler_params=pltpu.CompilerParams(
            dimension_semantics=("arbitrary",), vmem_limit_bytes=_vmem_limit(vmem)),
        name="mix",
    )(h, g, w_in, w_in, w_in, w_in, w_in, conv_w, conv_b, g_v, w_s, b_s_t, w_out, w_out)


def _kv_kernel(mem_ref, g_ref, wk_ref, wv_ref, k_ref, v_ref, mn_ref):
    @pl.when(pl.program_id(0) == 0)
    def _():
        mn_ref[...] = _rmsnorm(mem_ref[...], g_ref[...]).astype(BF16)

    mn = mn_ref[...]
    k_ref[...] = _dot(mn, wk_ref[...].astype(BF16)).astype(BF16)
    v_ref[...] = _dot(mn, wv_ref[...].astype(BF16)).astype(BF16)


def _kv(mem, g, w_k, w_v, *, tn):
    m, d = mem.shape
    vmem = (2 * m * d * 4 + m * d * 2 + 2 * 2 * d * tn * w_k.dtype.itemsize
            + 2 * 2 * m * tn * 2)
    return pl.pallas_call(
        _kv_kernel,
        grid=(d // tn,),
        in_specs=[
            pl.BlockSpec((m, d), lambda n: (0, 0)),
            pl.BlockSpec((1, d), lambda n: (0, 0)),
            pl.BlockSpec((d, tn), lambda n: (0, n)),
            pl.BlockSpec((d, tn), lambda n: (0, n)),
        ],
        out_specs=[pl.BlockSpec((m, tn), lambda n: (0, n))] * 2,
        out_shape=[jax.ShapeDtypeStruct((m, d), BF16)] * 2,
        scratch_shapes=[pltpu.VMEM((m, d), BF16)],
        compiler_params=pltpu.CompilerParams(
            dimension_semantics=("arbitrary",), vmem_limit_bytes=_vmem_limit(vmem)),
        name="kv",
    )(mem, g, w_k, w_v)


def _xattn_kernel(h_hbm, g_ref, wq_ref, k_ref, v_ref, wo_ref, o_hbm, buf_ref, hn_ref, q_ref,
                  in_sem, out_sem, *, n_tiles):
    s = pl.program_id(0)
    cur = lax.rem(s, 2)
    prev = 1 - cur
    tile_slot = lax.rem(s // XA_HEADS, 2)
    prev_slot = lax.rem(jnp.maximum(s - 1, 0) // XA_HEADS, 2)
    store = _stream_row_tiles(s, XA_HEADS, n_tiles, h_hbm, o_hbm, buf_ref, in_sem, out_sem)

    @pl.when(s == 0)
    def _():
        q_ref[prev] = jnp.zeros(q_ref.shape[1:], BF16)

    @pl.when((lax.rem(s, XA_HEADS) == 0) & (s < n_tiles * XA_HEADS))
    def _():
        hn_ref[...] = _rmsnorm(buf_ref[tile_slot], g_ref[...]).astype(BF16)

    q_prev = q_ref[prev]
    scores = lax.dot_general(q_prev, k_ref[...], (((1,), (1,)), ((), ())),
                             preferred_element_type=F32)
    q_ref[cur] = _dot(hn_ref[...], wq_ref[...].astype(BF16)).astype(BF16)
    scores = scores * (q_prev.shape[-1] ** -0.5)
    p = jnp.exp(scores - jnp.max(scores, axis=-1, keepdims=True))
    p = p / jnp.sum(p, axis=-1, keepdims=True)
    o = _dot(p.astype(BF16), v_ref[...])
    o = jnp.where(s > 0, o, 0.0)
    buf_ref[prev_slot] += _dot(o.astype(BF16), wo_ref[...].astype(BF16))

    @pl.when(s == n_tiles * XA_HEADS)
    def _():
        store(n_tiles - 1).start(priority=ROW_TILE_DMA_THREAD)
        store(n_tiles - 1).wait()


def _xattn(h, g, w_q, k, v, w_o, *, seq, n_mem, tm):
    m, d = h.shape
    hdim = d // XA_HEADS
    tiles_per_seq = seq // tm
    n_tiles = m // tm
    n_steps = n_tiles * XA_HEADS
    assert m % tm == 0 and seq % tm == 0 and XA_HEADS >= 3

    def prev_tile(s):
        return jnp.maximum(s - 1, 0) // XA_HEADS

    def prev_head(s):
        return jnp.maximum(s - 1, 0) % XA_HEADS

    vmem = (2 * tm * d * 4 + tm * d * 2 + 2 * tm * hdim * 2
            + 2 * 2 * d * hdim * w_q.dtype.itemsize + 2 * 2 * n_mem * hdim * 2)
    return pl.pallas_call(
        functools.partial(_xattn_kernel, n_tiles=n_tiles),
        grid=(n_steps + 1,),
        in_specs=[
            pl.BlockSpec(memory_space=pl.ANY),
            pl.BlockSpec((1, d), lambda s: (0, 0)),
            pl.BlockSpec((d, hdim), lambda s: (0, s % XA_HEADS)),
            pl.BlockSpec((n_mem, hdim), lambda s: (prev_tile(s) // tiles_per_seq, prev_head(s))),
            pl.BlockSpec((n_mem, hdim), lambda s: (prev_tile(s) // tiles_per_seq, prev_head(s))),
            pl.BlockSpec((hdim, d), lambda s: (prev_head(s), 0)),
        ],
        out_specs=pl.BlockSpec(memory_space=pl.ANY),
        out_shape=jax.ShapeDtypeStruct((m, d), F32),
        scratch_shapes=[pltpu.VMEM((2, tm, d), F32), pltpu.VMEM((tm, d), BF16),
                        pltpu.VMEM((2, tm, hdim), BF16),
                        pltpu.SemaphoreType.DMA(()), pltpu.SemaphoreType.DMA(())],
        compiler_params=pltpu.CompilerParams(
            dimension_semantics=("arbitrary",), vmem_limit_bytes=_vmem_limit(vmem)),
        name="xattn",
    )(h, g, w_q, k, v, w_o)


def kernel(x, mem, g_ffn1, w_ffn1_in, w_ffn1_out, g_mix, w_mix_in, conv_w, conv_b, g_gm_v,
           w_spatial, b_spatial, w_mix_out, g_xattn, g_mem, w_xq, w_xk, w_xv, w_xo, g_ffn2,
           w_ffn2_in, w_ffn2_out, g_final):
    b, s, d = x.shape
    n_mem = mem.shape[1]
    depth = g_ffn1.shape[0]
    assert depth >= 1
    heads_per_step = MIX_COLS // GROUP_DIM
    ffn_tiles = dict(tm=2048, tf=256, row_chunk=1024, unroll=1)
    tm = 1024

    def row(v):
        return v.reshape(1, -1)

    h = x.reshape(b * s, d)
    mem2 = mem.reshape(b * n_mem, d)
    g_fin = row(g_final)
    for l in range(depth):
        last = l == depth - 1
        h = _ffn(h, row(g_ffn1[l]), w_ffn1_in[l], w_ffn1_out[l], g_fin,
                 final_norm=False, **ffn_tiles)
        b_s_t = b_spatial[l].reshape(-1, heads_per_step, CHUNK).transpose(0, 2, 1)
        h = _mix(h, row(g_mix[l]), w_mix_in[l], conv_w[l], row(conv_b[l]), row(g_gm_v[l]),
                 w_spatial[l], b_s_t, w_mix_out[l], seq=s, tm=tm)
        k, v = _kv(mem2, row(g_mem[l]), w_xk[l], w_xv[l], tn=512)
        h = _xattn(h, row(g_xattn[l]), w_xq[l], k, v, w_xo[l], seq=s, n_mem=n_mem, tm=tm)
        h = _ffn(h, row(g_ffn2[l]), w_ffn2_in[l], w_ffn2_out[l], g_fin,
                 final_norm=last, **ffn_tiles)
    return h.reshape(b, s, d)
```

```python
import functools

import jax
import jax.numpy as jnp
from jax import lax
from jax.experimental import pallas as pl
from jax.experimental.pallas import tpu as pltpu

GROUP_DIM = 128
CHUNK = 128
CONV_K = 3
XA_HEADS = 4
EPS = 1e-6

V7X_VMEM_BYTES = 64 * 1024 * 1024
SUBLANES = 8
MIX_COLS = 256
ROW_TILE_DMA_THREAD = 1

BF16 = jnp.bfloat16
F32 = jnp.float32


def _rmsnorm(x, g):
    y = x * lax.rsqrt(jnp.mean(x * x, axis=-1, keepdims=True) + EPS)
    return y * g


_dot = functools.partial(jnp.dot, preferred_element_type=F32)


def _vmem_limit(nbytes):
    return min(int(nbytes * 1.25) + (8 << 20), V7X_VMEM_BYTES - (4 << 20))


def _ffn_kernel(x_hbm, g_ref, win_hbm, wout_hbm, gf_ref, o_hbm, buf_ref, xn_ref, wg_ref, wu_ref,
                wo_ref, in_sem, out_sem, w_sem, *, final_norm, row_chunk, unroll):
    i = pl.program_id(0)
    n_tiles = pl.num_programs(0)
    tm, d = xn_ref.shape
    tf = wo_ref.shape[1]
    d_ff = wout_hbm.shape[0]
    nk = d_ff // tf
    slot = lax.rem(i, 2)
    other = 1 - slot

    def load(tile, dst):
        return pltpu.make_async_copy(x_hbm.at[pl.ds(tile * tm, tm)], buf_ref.at[dst], in_sem)

    def store(tile, src):
        return pltpu.make_async_copy(buf_ref.at[src], o_hbm.at[pl.ds(tile * tm, tm)], out_sem)

    def weight_copies(k, ws):
        cols = pl.ds(pl.multiple_of(k * tf, tf), tf)
        up_cols = pl.ds(pl.multiple_of(d_ff + k * tf, tf), tf)
        return (pltpu.make_async_copy(win_hbm.at[:, cols], wg_ref.at[ws], w_sem.at[0, ws]),
                pltpu.make_async_copy(win_hbm.at[:, up_cols], wu_ref.at[ws], w_sem.at[1, ws]),
                pltpu.make_async_copy(wout_hbm.at[cols, :], wo_ref.at[ws], w_sem.at[2, ws]))

    @pl.when(i == 0)
    def _():
        load(0, 0).start(priority=ROW_TILE_DMA_THREAD)
        for c in weight_copies(0, 0):
            c.start()

    load(i, slot).wait()
    xn_ref[...] = _rmsnorm(buf_ref[slot], g_ref[...]).astype(BF16)

    @pl.when(i > 0)
    def _():
        store(i - 1, other).start(priority=ROW_TILE_DMA_THREAD)

    def block(k, carry):
        @pl.when(k == nk // 2)
        def _():
            @pl.when(i > 0)
            def _():
                store(i - 1, other).wait()

            @pl.when(i + 1 < n_tiles)
            def _():
                load(i + 1, other).start(priority=ROW_TILE_DMA_THREAD)

        ws = lax.rem(i * nk + k, 2)
        for c in weight_copies(k, ws):
            c.wait()

        for c in weight_copies(lax.rem(k + 1, nk), 1 - ws):
            c.start()

        w_gate, w_up, w_down = (r[ws].astype(BF16) for r in (wg_ref, wu_ref, wo_ref))
        for rows in (pl.ds(r * row_chunk, row_chunk) for r in range(tm // row_chunk)):
            xn = xn_ref[rows, :]
            gate = _dot(xn, w_gate)
            up = _dot(xn, w_up)
            hidden = (0.5 * gate * jax.nn.sigmoid(gate) * up).astype(BF16)
            buf_ref[slot, rows, :] += _dot(hidden, w_down)
        return carry

    lax.fori_loop(0, nk, block, 0, unroll=unroll)

    if final_norm:
        buf_ref[slot] = _rmsnorm(buf_ref[slot], gf_ref[...])

    @pl.when(i == n_tiles - 1)
    def _():
        store(i, slot).start(priority=ROW_TILE_DMA_THREAD)
        for c in weight_copies(0, lax.rem(n_tiles * nk, 2)):
            c.wait()
        store(i, slot).wait()


def _ffn(x, g, w_in, w_out, g_final, *, final_norm, tm, tf, row_chunk, unroll):
    m, d = x.shape
    d_ff = w_out.shape[0]
    nk = d_ff // tf
    assert m % tm == 0 and d_ff % tf == 0 and nk >= 2 and tm % row_chunk == 0
    vmem = (2 * tm * d * 4
            + tm * d * 2
            + 2 * 3 * d * tf * w_in.dtype.itemsize)
    return pl.pallas_call(
        functools.partial(_ffn_kernel, final_norm=final_norm, row_chunk=row_chunk, unroll=unroll),
        grid=(m // tm,),
        in_specs=[
            pl.BlockSpec(memory_space=pl.ANY),
            pl.BlockSpec((1, d), lambda i: (0, 0)),
            pl.BlockSpec(memory_space=pl.ANY),
            pl.BlockSpec(memory_space=pl.ANY),
            pl.BlockSpec((1, d), lambda i: (0, 0)),
        ],
        out_specs=pl.BlockSpec(memory_space=pl.ANY),
        out_shape=jax.ShapeDtypeStruct((m, d), F32),
        scratch_shapes=[pltpu.VMEM((2, tm, d), F32), pltpu.VMEM((tm, d), BF16),
                        pltpu.VMEM((2, d, tf), w_in.dtype), pltpu.VMEM((2, d, tf), w_in.dtype),
                        pltpu.VMEM((2, tf, d), w_out.dtype),
                        pltpu.SemaphoreType.DMA(()), pltpu.SemaphoreType.DMA(()),
                        pltpu.SemaphoreType.DMA((3, 2))],
        compiler_params=pltpu.CompilerParams(
            dimension_semantics=("arbitrary",), vmem_limit_bytes=_vmem_limit(vmem)),
        name="ffn_final" if final_norm else "ffn",
    )(x, g, w_in, w_out, g_final)


def _stream_row_tiles(s, per, n_tiles, h_hbm, o_hbm, buf_ref, in_sem, out_sem):
    tm = buf_ref.shape[1]
    t = s // per
    j = lax.rem(s, per)

    def load(tile):
        return pltpu.make_async_copy(h_hbm.at[pl.ds(tile * tm, tm)],
                                     buf_ref.at[lax.rem(tile, 2)], in_sem)

    def store(tile):
        return pltpu.make_async_copy(buf_ref.at[lax.rem(tile, 2)],
                                     o_hbm.at[pl.ds(tile * tm, tm)], out_sem)

    @pl.when(s == 0)
    def _():
        load(0).start(priority=ROW_TILE_DMA_THREAD)

    @pl.when((j == 0) & (t < n_tiles))
    def _():
        load(t).wait()

    @pl.when((j == 1) & (t >= 1))
    def _():
        store(t - 1).start(priority=ROW_TILE_DMA_THREAD)

    @pl.when(j == 2)
    def _():
        @pl.when(t >= 1)
        def _():
            store(t - 1).wait()

        @pl.when(t + 1 < n_tiles)
        def _():
            load(t + 1).start(priority=ROW_TILE_DMA_THREAD)

    return store


def _mix_kernel(h_hbm, g_ref, wb_ref, wc_ref, wh_ref, wu_ref, wv_ref, cw_ref, cb_ref,
                gv_ref, ws_ref, bs_ref, woc_ref, wog_ref, o_hbm, buf_ref, hn_ref, y_ref, carry_ref,
                in_sem, out_sem, *, nj, n_tiles, tiles_per_seq):
    s = pl.program_id(0)
    j = lax.rem(s, nj)
    cur = lax.rem(s, 2)
    prev = 1 - cur
    tm = hn_ref.shape[0]
    tile_slot = lax.rem(s // nj, 2)
    prev_slot = lax.rem(jnp.maximum(s - 1, 0) // nj, 2)
    store = _stream_row_tiles(s, nj, n_tiles, h_hbm, o_hbm, buf_ref, in_sem, out_sem)

    @pl.when(s == 0)
    def _():
        y_ref[prev] = jnp.zeros(y_ref.shape[1:], BF16)

    @pl.when((j == 0) & (s < n_tiles * nj))
    def _():
        hn_ref[...] = _rmsnorm(buf_ref[tile_slot], g_ref[...]).astype(BF16)

    @pl.when(lax.rem(s // nj, tiles_per_seq) == 0)
    def _():
        carry_ref[j] = jnp.zeros((SUBLANES, MIX_COLS), F32)

    w_in = jnp.concatenate([r[...].astype(BF16) for r in (wv_ref, wu_ref, wc_ref, wh_ref, wb_ref)],
                           axis=1)
    proj = _dot(hn_ref[...], w_in)
    v, u, gate_c, h_c, gate_b = (proj[:, n * MIX_COLS:(n + 1) * MIX_COLS] for n in range(5))

    y_prev = y_ref[prev]
    buf_ref[prev_slot] += _dot(y_prev[:, :MIX_COLS], woc_ref[...].astype(BF16))

    gv = gv_ref[...]
    bs = bs_ref[...]
    tri = (lax.broadcasted_iota(jnp.int32, (CHUNK, CHUNK), 0)
           >= lax.broadcasted_iota(jnp.int32, (CHUNK, CHUNK), 1))
    sgs = []
    for hh in range(MIX_COLS // GROUP_DIM):
        lanes = slice(hh * GROUP_DIM, (hh + 1) * GROUP_DIM)
        vh = _rmsnorm(v[:, lanes], gv[:, lanes]).astype(BF16)
        w = jnp.where(tri, ws_ref[hh], 0.0).astype(BF16)
        sgs.append([_dot(w, vh[c * CHUNK:(c + 1) * CHUNK]) for c in range(tm // CHUNK)])

    buf_ref[prev_slot] += _dot(y_prev[:, MIX_COLS:], wog_ref[...].astype(BF16))

    for hh, per_chunk in enumerate(sgs):
        lanes = slice(hh * GROUP_DIM, (hh + 1) * GROUP_DIM)
        for c, sg in enumerate(per_chunk):
            rows = slice(c * CHUNK, (c + 1) * CHUNK)
            y_ref[cur, rows, MIX_COLS + hh * GROUP_DIM:MIX_COLS + (hh + 1) * GROUP_DIM] = (
                u[rows, lanes] * (sg + bs[:, hh:hh + 1])).astype(BF16)

    z = gate_c * h_c
    tail = carry_ref[j]
    carry_ref[j] = z[tm - SUBLANES:, :]
    cw = cw_ref[...]
    cb = cb_ref[...]
    row = lax.broadcasted_iota(jnp.int32, (SUBLANES, MIX_COLS), 0)
    z1 = pltpu.roll(z, 1, 0)
    z2 = pltpu.roll(z, 2, 0)
    head1 = jnp.where(row < 1, pltpu.roll(tail, 1, 0), z1[:SUBLANES])
    head2 = jnp.where(row < 2, pltpu.roll(tail, 2, 0), z2[:SUBLANES])
    z1 = jnp.concatenate([head1, z1[SUBLANES:]], axis=0)
    z2 = jnp.concatenate([head2, z2[SUBLANES:]], axis=0)
    y_ref[cur, :, :MIX_COLS] = (
        gate_b * (cb + cw[2:3] * z + cw[1:2] * z1 + cw[0:1] * z2)).astype(BF16)

    @pl.when(s == n_tiles * nj)
    def _():
        store(n_tiles - 1).start(priority=ROW_TILE_DMA_THREAD)
        store(n_tiles - 1).wait()


def _mix(h, g, w_in, conv_w, conv_b, g_v, w_s, b_s_t, w_out, *, seq, tm):
    m, d = h.shape
    width = w_out.shape[0]
    conv_width = width // 2
    nj = conv_width // MIX_COLS
    heads_per_step = MIX_COLS // GROUP_DIM
    n_tiles = m // tm
    n_steps = n_tiles * nj
    assert m % tm == 0 and seq % tm == 0 and tm % CHUNK == 0 and nj >= 3

    def prev_j(s):
        return jnp.maximum(s - 1, 0) % nj

    def col_block(offset):
        return pl.BlockSpec((d, MIX_COLS), lambda s: (0, offset + s % nj))

    vmem = (2 * tm * d * 4 + tm * d * 2 + 2 * tm * 2 * MIX_COLS * 2
            + 2 * 5 * d * MIX_COLS * w_in.dtype.itemsize
            + 2 * 2 * MIX_COLS * d * w_out.dtype.itemsize)
    return pl.pallas_call(
        functools.partial(_mix_kernel, nj=nj, n_tiles=n_tiles, tiles_per_seq=seq // tm),
        grid=(n_steps + 1,),
        in_specs=[
            pl.BlockSpec(memory_space=pl.ANY),
            pl.BlockSpec((1, d), lambda s: (0, 0)),
            col_block(0), col_block(nj), col_block(2 * nj),
            col_block(3 * nj), col_block(4 * nj),
            pl.BlockSpec((CONV_K, MIX_COLS), lambda s: (0, s % nj)),
            pl.BlockSpec((1, MIX_COLS), lambda s: (0, s % nj)),
            pl.BlockSpec((1, MIX_COLS), lambda s: (0, s % nj)),
            pl.BlockSpec((heads_per_step, CHUNK, CHUNK), lambda s: (s % nj, 0, 0)),
            pl.BlockSpec((None, CHUNK, heads_per_step), lambda s: (s % nj, 0, 0)),
            pl.BlockSpec((MIX_COLS, d), lambda s: (prev_j(s), 0)),
            pl.BlockSpec((MIX_COLS, d), lambda s: (nj + prev_j(s), 0)),
        ],
        out_specs=pl.BlockSpec(memory_space=pl.ANY),
        out_shape=jax.ShapeDtypeStruct((m, d), F32),
        scratch_shapes=[pltpu.VMEM((2, tm, d), F32), pltpu.VMEM((tm, d), BF16),
                        pltpu.VMEM((2, tm, 2 * MIX_COLS), BF16),
                        pltpu.VMEM((nj, SUBLANES, MIX_COLS), F32),
                        pltpu.SemaphoreType.DMA(()), pltpu.SemaphoreType.DMA(())],
        compiler_params=pltpu.CompilerParams(
            dimension_semantics=("arbitrary",), vmem_limit_bytes=_vmem_limit(vmem)),
        name="mix",
    )(h, g, w_in, w_in, w_in, w_in, w_in, conv_w, conv_b, g_v, w_s, b_s_t, w_out, w_out)


def _kv_kernel(mem_ref, g_ref, wk_ref, wv_ref, k_ref, v_ref, mn_ref):
    @pl.when(pl.program_id(0) == 0)
    def _():
        mn_ref[...] = _rmsnorm(mem_ref[...], g_ref[...]).astype(BF16)

    mn = mn_ref[...]
    k_ref[...] = _dot(mn, wk_ref[...].astype(BF16)).astype(BF16)
    v_ref[...] = _dot(mn, wv_ref[...].astype(BF16)).astype(BF16)


def _kv(mem, g, w_k, w_v, *, tn):
    m, d = mem.shape
    vmem = (2 * m * d * 4 + m * d * 2 + 2 * 2 * d * tn * w_k.dtype.itemsize
            + 2 * 2 * m * tn * 2)
    return pl.pallas_call(
        _kv_kernel,
        grid=(d // tn,),
        in_specs=[
            pl.BlockSpec((m, d), lambda n: (0, 0)),
            pl.BlockSpec((1, d), lambda n: (0, 0)),
            pl.BlockSpec((d, tn), lambda n: (0, n)),
            pl.BlockSpec((d, tn), lambda n: (0, n)),
        ],
        out_specs=[pl.BlockSpec((m, tn), lambda n: (0, n))] * 2,
        out_shape=[jax.ShapeDtypeStruct((m, d), BF16)] * 2,
        scratch_shapes=[pltpu.VMEM((m, d), BF16)],
        compiler_params=pltpu.CompilerParams(
            dimension_semantics=("arbitrary",), vmem_limit_bytes=_vmem_limit(vmem)),
        name="kv",
    )(mem, g, w_k, w_v)


def _xattn_kernel(h_hbm, g_ref, wq_ref, k_ref, v_ref, wo_ref, o_hbm, buf_ref, hn_ref, q_ref,
                  wq_all, wo_all, in_sem, out_sem, *, n_tiles):
    s = pl.program_id(0)
    cur = lax.rem(s, 2)
    prev = 1 - cur
    head = lax.rem(s, XA_HEADS)
    prev_head = lax.rem(jnp.maximum(s - 1, 0), XA_HEADS)
    tile_slot = lax.rem(s // XA_HEADS, 2)
    prev_slot = lax.rem(jnp.maximum(s - 1, 0) // XA_HEADS, 2)
    store = _stream_row_tiles(s, XA_HEADS, n_tiles, h_hbm, o_hbm, buf_ref, in_sem, out_sem)

    @pl.when(s == 0)
    def _():
        q_ref[prev] = jnp.zeros(q_ref.shape[1:], BF16)

    @pl.when(s < XA_HEADS)
    def _():
        wq_all[head] = wq_ref[...].astype(BF16)

    @pl.when(s <= XA_HEADS)
    def _():
        wo_all[prev_head] = wo_ref[...].astype(BF16)

    @pl.when((lax.rem(s, XA_HEADS) == 0) & (s < n_tiles * XA_HEADS))
    def _():
        hn_ref[...] = _rmsnorm(buf_ref[tile_slot], g_ref[...]).astype(BF16)

    q_prev = q_ref[prev]
    scores = lax.dot_general(q_prev, k_ref[...], (((1,), (1,)), ((), ())),
                             preferred_element_type=F32)
    q_ref[cur] = _dot(hn_ref[...], wq_all[head]).astype(BF16)
    scores = scores * (q_prev.shape[-1] ** -0.5)
    p = jnp.exp(scores - jnp.max(scores, axis=-1, keepdims=True))
    p = p / jnp.sum(p, axis=-1, keepdims=True)
    o = _dot(p.astype(BF16), v_ref[...])
    o = jnp.where(s > 0, o, 0.0)
    buf_ref[prev_slot] += _dot(o.astype(BF16), wo_all[prev_head])

    @pl.when(s == n_tiles * XA_HEADS)
    def _():
        store(n_tiles - 1).start(priority=ROW_TILE_DMA_THREAD)
        store(n_tiles - 1).wait()


def _xattn(h, g, w_q, k, v, w_o, *, seq, n_mem, tm):
    m, d = h.shape
    hdim = d // XA_HEADS
    tiles_per_seq = seq // tm
    n_tiles = m // tm
    n_steps = n_tiles * XA_HEADS
    assert m % tm == 0 and seq % tm == 0 and XA_HEADS >= 3

    def prev_tile(s):
        return jnp.maximum(s - 1, 0) // XA_HEADS

    def prev_head(s):
        return jnp.maximum(s - 1, 0) % XA_HEADS

    vmem = (2 * tm * d * 4 + tm * d * 2 + 2 * tm * hdim * 2
            + 2 * 2 * d * hdim * w_q.dtype.itemsize + 2 * 2 * n_mem * hdim * 2
            + 2 * d * d * 2)
    return pl.pallas_call(
        functools.partial(_xattn_kernel, n_tiles=n_tiles),
        grid=(n_steps + 1,),
        in_specs=[
            pl.BlockSpec(memory_space=pl.ANY),
            pl.BlockSpec((1, d), lambda s: (0, 0)),
            pl.BlockSpec((d, hdim), lambda s: (0, jnp.minimum(s, XA_HEADS - 1))),
            pl.BlockSpec((n_mem, hdim), lambda s: (prev_tile(s) // tiles_per_seq, prev_head(s))),
            pl.BlockSpec((n_mem, hdim), lambda s: (prev_tile(s) // tiles_per_seq, prev_head(s))),
            pl.BlockSpec((hdim, d), lambda s: (prev_head(jnp.minimum(s, XA_HEADS)), 0)),
        ],
        out_specs=pl.BlockSpec(memory_space=pl.ANY),
        out_shape=jax.ShapeDtypeStruct((m, d), F32),
        scratch_shapes=[pltpu.VMEM((2, tm, d), F32), pltpu.VMEM((tm, d), BF16),
                        pltpu.VMEM((2, tm, hdim), BF16),
                        pltpu.VMEM((XA_HEADS, d, hdim), BF16), pltpu.VMEM((XA_HEADS, hdim, d), BF16),
                        pltpu.SemaphoreType.DMA(()), pltpu.SemaphoreType.DMA(())],
        compiler_params=pltpu.CompilerParams(
            dimension_semantics=("arbitrary",), vmem_limit_bytes=_vmem_limit(vmem)),
        name="xattn",
    )(h, g, w_q, k, v, w_o)


def kernel(x, mem, g_ffn1, w_ffn1_in, w_ffn1_out, g_mix, w_mix_in, conv_w, conv_b, g_gm_v,
           w_spatial, b_spatial, w_mix_out, g_xattn, g_mem, w_xq, w_xk, w_xv, w_xo, g_ffn2,
           w_ffn2_in, w_ffn2_out, g_final):
    b, s, d = x.shape
    n_mem = mem.shape[1]
    depth = g_ffn1.shape[0]
    assert depth >= 1
    heads_per_step = MIX_COLS // GROUP_DIM
    ffn_tiles = dict(tm=2048, tf=256, row_chunk=1024, unroll=1)
    tm = 1024

    def row(v):
        return v.reshape(1, -1)

    h = x.reshape(b * s, d)
    mem2 = mem.reshape(b * n_mem, d)
    g_fin = row(g_final)
    for l in range(depth):
        last = l == depth - 1
        h = _ffn(h, row(g_ffn1[l]), w_ffn1_in[l], w_ffn1_out[l], g_fin,
                 final_norm=False, **ffn_tiles)
        b_s_t = b_spatial[l].reshape(-1, heads_per_step, CHUNK).transpose(0, 2, 1)
        h = _mix(h, row(g_mix[l]), w_mix_in[l], conv_w[l], row(conv_b[l]), row(g_gm_v[l]),
                 w_spatial[l], b_s_t, w_mix_out[l], seq=s, tm=tm)
        k, v = _kv(mem2, row(g_mem[l]), w_xk[l], w_xv[l], tn=512)
        h = _xattn(h, row(g_xattn[l]), w_xq[l], k, v, w_xo[l], seq=s, n_mem=n_mem, tm=tm)
        h = _ffn(h, row(g_ffn2[l]), w_ffn2_in[l], w_ffn2_out[l], g_fin,
                 final_norm=last, **ffn_tiles)
    return h.reshape(b, s, d)
```

```python
import functools

import jax
import jax.numpy as jnp
from jax import lax
from jax.experimental import pallas as pl
from jax.experimental.pallas import tpu as pltpu

GROUP_DIM = 128
CHUNK = 128
CONV_K = 3
XA_HEADS = 4
EPS = 1e-6

V7X_VMEM_BYTES = 64 * 1024 * 1024
SUBLANES = 8
MIX_COLS = 256
ROW_TILE_DMA_THREAD = 1

BF16 = jnp.bfloat16
F32 = jnp.float32


def _rmsnorm(x, g):
    y = x * lax.rsqrt(jnp.mean(x * x, axis=-1, keepdims=True) + EPS)
    return y * g


_dot = functools.partial(jnp.dot, preferred_element_type=F32)


TEMPORARIES_BYTES = 8 << 20
VMEM_RESERVE_BYTES = 4 << 20


def _vmem_limit(buffer_bytes):
    return min(buffer_bytes + buffer_bytes // 4 + TEMPORARIES_BYTES,
               V7X_VMEM_BYTES - VMEM_RESERVE_BYTES)


def _ffn_kernel(x_hbm, g_ref, win_hbm, wout_hbm, gf_ref, o_hbm, buf_ref, xn_ref, wg_ref, wu_ref,
                wo_ref, in_sem, out_sem, w_sem, *, final_norm, row_chunk):
    i = pl.program_id(0)
    n_tiles = pl.num_programs(0)
    tm, d = xn_ref.shape
    tf = wo_ref.shape[1]
    d_ff = wout_hbm.shape[0]
    nk = d_ff // tf
    slot = lax.rem(i, 2)
    other = 1 - slot

    def load(tile, dst):
        return pltpu.make_async_copy(x_hbm.at[pl.ds(tile * tm, tm)], buf_ref.at[dst], in_sem)

    def store(tile, src):
        return pltpu.make_async_copy(buf_ref.at[src], o_hbm.at[pl.ds(tile * tm, tm)], out_sem)

    def weight_copies(k, ws):
        cols = pl.ds(pl.multiple_of(k * tf, tf), tf)
        up_cols = pl.ds(pl.multiple_of(d_ff + k * tf, tf), tf)
        return (pltpu.make_async_copy(win_hbm.at[:, cols], wg_ref.at[ws], w_sem.at[0, ws]),
                pltpu.make_async_copy(win_hbm.at[:, up_cols], wu_ref.at[ws], w_sem.at[1, ws]),
                pltpu.make_async_copy(wout_hbm.at[cols, :], wo_ref.at[ws], w_sem.at[2, ws]))

    @pl.when(i == 0)
    def _():
        load(0, 0).start(priority=ROW_TILE_DMA_THREAD)
        for c in weight_copies(0, 0):
            c.start()

    load(i, slot).wait()
    xn_ref[...] = _rmsnorm(buf_ref[slot], g_ref[...]).astype(BF16)

    @pl.when(i > 0)
    def _():
        store(i - 1, other).start(priority=ROW_TILE_DMA_THREAD)

    def block(k, carry):
        @pl.when(k == nk // 2)
        def _():
            @pl.when(i > 0)
            def _():
                store(i - 1, other).wait()

            @pl.when(i + 1 < n_tiles)
            def _():
                load(i + 1, other).start(priority=ROW_TILE_DMA_THREAD)

        ws = lax.rem(i * nk + k, 2)
        for c in weight_copies(k, ws):
            c.wait()

        for c in weight_copies(lax.rem(k + 1, nk), 1 - ws):
            c.start()

        w_gate, w_up, w_down = (r[ws].astype(BF16) for r in (wg_ref, wu_ref, wo_ref))
        for rows in (pl.ds(r * row_chunk, row_chunk) for r in range(tm // row_chunk)):
            xn = xn_ref[rows, :]
            gate = _dot(xn, w_gate)
            up = _dot(xn, w_up)
            hidden = (0.5 * gate * jax.nn.sigmoid(gate) * up).astype(BF16)
            buf_ref[slot, rows, :] += _dot(hidden, w_down)
        return carry

    lax.fori_loop(0, nk, block, 0)

    if final_norm:
        buf_ref[slot] = _rmsnorm(buf_ref[slot], gf_ref[...])

    @pl.when(i == n_tiles - 1)
    def _():
        store(i, slot).start(priority=ROW_TILE_DMA_THREAD)
        for c in weight_copies(0, lax.rem(n_tiles * nk, 2)):
            c.wait()
        store(i, slot).wait()


def _ffn(x, g, w_in, w_out, g_final, *, final_norm, tm, tf, row_chunk):
    m, d = x.shape
    d_ff = w_out.shape[0]
    nk = d_ff // tf
    assert m % tm == 0 and d_ff % tf == 0 and nk >= 2 and tm % row_chunk == 0
    vmem = (2 * tm * d * 4
            + tm * d * 2
            + 2 * 3 * d * tf * w_in.dtype.itemsize)
    return pl.pallas_call(
        functools.partial(_ffn_kernel, final_norm=final_norm, row_chunk=row_chunk),
        grid=(m // tm,),
        in_specs=[
            pl.BlockSpec(memory_space=pl.ANY),
            pl.BlockSpec((1, d), lambda i: (0, 0)),
            pl.BlockSpec(memory_space=pl.ANY),
            pl.BlockSpec(memory_space=pl.ANY),
            pl.BlockSpec((1, d), lambda i: (0, 0)),
        ],
        out_specs=pl.BlockSpec(memory_space=pl.ANY),
        out_shape=jax.ShapeDtypeStruct((m, d), F32),
        scratch_shapes=[pltpu.VMEM((2, tm, d), F32), pltpu.VMEM((tm, d), BF16),
                        pltpu.VMEM((2, d, tf), w_in.dtype), pltpu.VMEM((2, d, tf), w_in.dtype),
                        pltpu.VMEM((2, tf, d), w_out.dtype),
                        pltpu.SemaphoreType.DMA(()), pltpu.SemaphoreType.DMA(()),
                        pltpu.SemaphoreType.DMA((3, 2))],
        compiler_params=pltpu.CompilerParams(
            dimension_semantics=("arbitrary",), vmem_limit_bytes=_vmem_limit(vmem)),
        name="ffn_final" if final_norm else "ffn",
    )(x, g, w_in, w_out, g_final)


def _stream_row_tiles(s, per, n_tiles, h_hbm, o_hbm, buf_ref, in_sem, out_sem):
    tm = buf_ref.shape[1]
    t = s // per
    j = lax.rem(s, per)

    def load(tile):
        return pltpu.make_async_copy(h_hbm.at[pl.ds(tile * tm, tm)],
                                     buf_ref.at[lax.rem(tile, 2)], in_sem)

    def store(tile):
        return pltpu.make_async_copy(buf_ref.at[lax.rem(tile, 2)],
                                     o_hbm.at[pl.ds(tile * tm, tm)], out_sem)

    @pl.when(s == 0)
    def _():
        load(0).start(priority=ROW_TILE_DMA_THREAD)

    @pl.when((j == 0) & (t < n_tiles))
    def _():
        load(t).wait()

    @pl.when((j == 1) & (t >= 1))
    def _():
        store(t - 1).start(priority=ROW_TILE_DMA_THREAD)

    @pl.when(j == 2)
    def _():
        @pl.when(t >= 1)
        def _():
            store(t - 1).wait()

        @pl.when(t + 1 < n_tiles)
        def _():
            load(t + 1).start(priority=ROW_TILE_DMA_THREAD)

    return store


def _mix_kernel(h_hbm, g_ref, wb_ref, wc_ref, wh_ref, wu_ref, wv_ref, cw_ref, cb_ref,
                gv_ref, ws_ref, bs_ref, woc_ref, wog_ref, o_hbm, buf_ref, hn_ref, y_ref, carry_ref,
                in_sem, out_sem, *, nj, n_tiles, tiles_per_seq):
    s = pl.program_id(0)
    j = lax.rem(s, nj)
    cur = lax.rem(s, 2)
    prev = 1 - cur
    tm = hn_ref.shape[0]
    tile_slot = lax.rem(s // nj, 2)
    prev_slot = lax.rem(jnp.maximum(s - 1, 0) // nj, 2)
    store = _stream_row_tiles(s, nj, n_tiles, h_hbm, o_hbm, buf_ref, in_sem, out_sem)

    @pl.when(s == 0)
    def _():
        y_ref[prev] = jnp.zeros(y_ref.shape[1:], BF16)

    @pl.when((j == 0) & (s < n_tiles * nj))
    def _():
        hn_ref[...] = _rmsnorm(buf_ref[tile_slot], g_ref[...]).astype(BF16)

    @pl.when(lax.rem(s // nj, tiles_per_seq) == 0)
    def _():
        carry_ref[j] = jnp.zeros((SUBLANES, MIX_COLS), F32)

    w_in = jnp.concatenate([r[...].astype(BF16) for r in (wv_ref, wu_ref, wc_ref, wh_ref, wb_ref)],
                           axis=1)
    proj = _dot(hn_ref[...], w_in)
    v, u, gate_c, h_c, gate_b = (proj[:, n * MIX_COLS:(n + 1) * MIX_COLS] for n in range(5))

    y_prev = y_ref[prev]
    buf_ref[prev_slot] += _dot(y_prev[:, :MIX_COLS], woc_ref[...].astype(BF16))

    gv = gv_ref[...]
    bs = bs_ref[...]
    tri = (lax.broadcasted_iota(jnp.int32, (CHUNK, CHUNK), 0)
           >= lax.broadcasted_iota(jnp.int32, (CHUNK, CHUNK), 1))
    sgs = []
    for hh in range(MIX_COLS // GROUP_DIM):
        lanes = slice(hh * GROUP_DIM, (hh + 1) * GROUP_DIM)
        vh = _rmsnorm(v[:, lanes], gv[:, lanes]).astype(BF16)
        w = jnp.where(tri, ws_ref[hh], 0.0).astype(BF16)
        sgs.append([_dot(w, vh[c * CHUNK:(c + 1) * CHUNK]) for c in range(tm // CHUNK)])

    buf_ref[prev_slot] += _dot(y_prev[:, MIX_COLS:], wog_ref[...].astype(BF16))

    for hh, per_chunk in enumerate(sgs):
        lanes = slice(hh * GROUP_DIM, (hh + 1) * GROUP_DIM)
        for c, sg in enumerate(per_chunk):
            rows = slice(c * CHUNK, (c + 1) * CHUNK)
            y_ref[cur, rows, MIX_COLS + hh * GROUP_DIM:MIX_COLS + (hh + 1) * GROUP_DIM] = (
                u[rows, lanes] * (sg + bs[:, hh:hh + 1])).astype(BF16)

    z = gate_c * h_c
    tail = carry_ref[j]
    carry_ref[j] = z[tm - SUBLANES:, :]
    cw = cw_ref[...]
    cb = cb_ref[...]
    row = lax.broadcasted_iota(jnp.int32, (SUBLANES, MIX_COLS), 0)
    z1 = pltpu.roll(z, 1, 0)
    z2 = pltpu.roll(z, 2, 0)
    head1 = jnp.where(row < 1, pltpu.roll(tail, 1, 0), z1[:SUBLANES])
    head2 = jnp.where(row < 2, pltpu.roll(tail, 2, 0), z2[:SUBLANES])
    z1 = jnp.concatenate([head1, z1[SUBLANES:]], axis=0)
    z2 = jnp.concatenate([head2, z2[SUBLANES:]], axis=0)
    y_ref[cur, :, :MIX_COLS] = (
        gate_b * (cb + cw[2:3] * z + cw[1:2] * z1 + cw[0:1] * z2)).astype(BF16)

    @pl.when(s == n_tiles * nj)
    def _():
        store(n_tiles - 1).start(priority=ROW_TILE_DMA_THREAD)
        store(n_tiles - 1).wait()


def _mix(h, g, w_in, conv_w, conv_b, g_v, w_s, b_s_t, w_out, *, seq, tm):
    m, d = h.shape
    width = w_out.shape[0]
    conv_width = width // 2
    nj = conv_width // MIX_COLS
    heads_per_step = MIX_COLS // GROUP_DIM
    n_tiles = m // tm
    n_steps = n_tiles * nj
    assert m % tm == 0 and seq % tm == 0 and tm % CHUNK == 0 and nj >= 3

    def prev_j(s):
        return jnp.maximum(s - 1, 0) % nj

    def col_block(offset):
        return pl.BlockSpec((d, MIX_COLS), lambda s: (0, offset + s % nj))

    vmem = (2 * tm * d * 4 + tm * d * 2 + 2 * tm * 2 * MIX_COLS * 2
            + 2 * 5 * d * MIX_COLS * w_in.dtype.itemsize
            + 2 * 2 * MIX_COLS * d * w_out.dtype.itemsize)
    return pl.pallas_call(
        functools.partial(_mix_kernel, nj=nj, n_tiles=n_tiles, tiles_per_seq=seq // tm),
        grid=(n_steps + 1,),
        in_specs=[
            pl.BlockSpec(memory_space=pl.ANY),
            pl.BlockSpec((1, d), lambda s: (0, 0)),
            col_block(0), col_block(nj), col_block(2 * nj),
            col_block(3 * nj), col_block(4 * nj),
            pl.BlockSpec((CONV_K, MIX_COLS), lambda s: (0, s % nj)),
            pl.BlockSpec((1, MIX_COLS), lambda s: (0, s % nj)),
            pl.BlockSpec((1, MIX_COLS), lambda s: (0, s % nj)),
            pl.BlockSpec((heads_per_step, CHUNK, CHUNK), lambda s: (s % nj, 0, 0)),
            pl.BlockSpec((None, CHUNK, heads_per_step), lambda s: (s % nj, 0, 0)),
            pl.BlockSpec((MIX_COLS, d), lambda s: (prev_j(s), 0)),
            pl.BlockSpec((MIX_COLS, d), lambda s: (nj + prev_j(s), 0)),
        ],
        out_specs=pl.BlockSpec(memory_space=pl.ANY),
        out_shape=jax.ShapeDtypeStruct((m, d), F32),
        scratch_shapes=[pltpu.VMEM((2, tm, d), F32), pltpu.VMEM((tm, d), BF16),
                        pltpu.VMEM((2, tm, 2 * MIX_COLS), BF16),
                        pltpu.VMEM((nj, SUBLANES, MIX_COLS), F32),
                        pltpu.SemaphoreType.DMA(()), pltpu.SemaphoreType.DMA(())],
        compiler_params=pltpu.CompilerParams(
            dimension_semantics=("arbitrary",), vmem_limit_bytes=_vmem_limit(vmem)),
        name="mix",
    )(h, g, w_in, w_in, w_in, w_in, w_in, conv_w, conv_b, g_v, w_s, b_s_t, w_out, w_out)


def _kv_kernel(mem_ref, g_ref, wk_ref, wv_ref, k_ref, v_ref, mn_ref):
    @pl.when(pl.program_id(0) == 0)
    def _():
        mn_ref[...] = _rmsnorm(mem_ref[...], g_ref[...]).astype(BF16)

    mn = mn_ref[...]
    k_ref[...] = _dot(mn, wk_ref[...].astype(BF16)).astype(BF16)
    v_ref[...] = _dot(mn, wv_ref[...].astype(BF16)).astype(BF16)


def _kv(mem, g, w_k, w_v, *, tn):
    m, d = mem.shape
    vmem = (2 * m * d * 4 + m * d * 2 + 2 * 2 * d * tn * w_k.dtype.itemsize
            + 2 * 2 * m * tn * 2)
    return pl.pallas_call(
        _kv_kernel,
        grid=(d // tn,),
        in_specs=[
            pl.BlockSpec((m, d), lambda n: (0, 0)),
            pl.BlockSpec((1, d), lambda n: (0, 0)),
            pl.BlockSpec((d, tn), lambda n: (0, n)),
            pl.BlockSpec((d, tn), lambda n: (0, n)),
        ],
        out_specs=[pl.BlockSpec((m, tn), lambda n: (0, n))] * 2,
        out_shape=[jax.ShapeDtypeStruct((m, d), BF16)] * 2,
        scratch_shapes=[pltpu.VMEM((m, d), BF16)],
        compiler_params=pltpu.CompilerParams(
            dimension_semantics=("arbitrary",), vmem_limit_bytes=_vmem_limit(vmem)),
        name="kv",
    )(mem, g, w_k, w_v)


def _xattn_kernel(h_hbm, g_ref, wq_ref, k_ref, v_ref, wo_ref, o_hbm, buf_ref, hn_ref, q_ref,
                  wq_all, wo_all, in_sem, out_sem, *, n_tiles):
    s = pl.program_id(0)
    cur = lax.rem(s, 2)
    prev = 1 - cur
    head = lax.rem(s, XA_HEADS)
    prev_head = lax.rem(jnp.maximum(s - 1, 0), XA_HEADS)
    tile_slot = lax.rem(s // XA_HEADS, 2)
    prev_slot = lax.rem(jnp.maximum(s - 1, 0) // XA_HEADS, 2)
    store = _stream_row_tiles(s, XA_HEADS, n_tiles, h_hbm, o_hbm, buf_ref, in_sem, out_sem)

    @pl.when(s == 0)
    def _():
        q_ref[prev] = jnp.zeros(q_ref.shape[1:], BF16)

    @pl.when(s < XA_HEADS)
    def _():
        wq_all[head] = wq_ref[...].astype(BF16)

    @pl.when(s <= XA_HEADS)
    def _():
        wo_all[prev_head] = wo_ref[...].astype(BF16)

    @pl.when((lax.rem(s, XA_HEADS) == 0) & (s < n_tiles * XA_HEADS))
    def _():
        hn_ref[...] = _rmsnorm(buf_ref[tile_slot], g_ref[...]).astype(BF16)

    q_prev = q_ref[prev]
    scores = lax.dot_general(q_prev, k_ref[...], (((1,), (1,)), ((), ())),
                             preferred_element_type=F32)
    q_ref[cur] = _dot(hn_ref[...], wq_all[head]).astype(BF16)
    scores = scores * (q_prev.shape[-1] ** -0.5)
    p = jnp.exp(scores - jnp.max(scores, axis=-1, keepdims=True))
    p = p / jnp.sum(p, axis=-1, keepdims=True)
    o = _dot(p.astype(BF16), v_ref[...])
    o = jnp.where(s > 0, o, 0.0)
    buf_ref[prev_slot] += _dot(o.astype(BF16), wo_all[prev_head])

    @pl.when(s == n_tiles * XA_HEADS)
    def _():
        store(n_tiles - 1).start(priority=ROW_TILE_DMA_THREAD)
        store(n_tiles - 1).wait()


def _xattn(h, g, w_q, k, v, w_o, *, seq, n_mem, tm):
    m, d = h.shape
    hdim = d // XA_HEADS
    tiles_per_seq = seq // tm
    n_tiles = m // tm
    n_steps = n_tiles * XA_HEADS
    assert m % tm == 0 and seq % tm == 0 and XA_HEADS >= 3

    def prev_tile(s):
        return jnp.maximum(s - 1, 0) // XA_HEADS

    def prev_head(s):
        return jnp.maximum(s - 1, 0) % XA_HEADS

    vmem = (2 * tm * d * 4 + tm * d * 2 + 2 * tm * hdim * 2
            + 2 * 2 * d * hdim * w_q.dtype.itemsize + 2 * 2 * n_mem * hdim * 2
            + 2 * d * d * 2)
    return pl.pallas_call(
        functools.partial(_xattn_kernel, n_tiles=n_tiles),
        grid=(n_steps + 1,),
        in_specs=[
            pl.BlockSpec(memory_space=pl.ANY),
            pl.BlockSpec((1, d), lambda s: (0, 0)),
            pl.BlockSpec((d, hdim), lambda s: (0, jnp.minimum(s, XA_HEADS - 1))),
            pl.BlockSpec((n_mem, hdim), lambda s: (prev_tile(s) // tiles_per_seq, prev_head(s))),
            pl.BlockSpec((n_mem, hdim), lambda s: (prev_tile(s) // tiles_per_seq, prev_head(s))),
            pl.BlockSpec((hdim, d), lambda s: (prev_head(jnp.minimum(s, XA_HEADS)), 0)),
        ],
        out_specs=pl.BlockSpec(memory_space=pl.ANY),
        out_shape=jax.ShapeDtypeStruct((m, d), F32),
        scratch_shapes=[pltpu.VMEM((2, tm, d), F32), pltpu.VMEM((tm, d), BF16),
                        pltpu.VMEM((2, tm, hdim), BF16),
                        pltpu.VMEM((XA_HEADS, d, hdim), BF16), pltpu.VMEM((XA_HEADS, hdim, d), BF16),
                        pltpu.SemaphoreType.DMA(()), pltpu.SemaphoreType.DMA(())],
        compiler_params=pltpu.CompilerParams(
            dimension_semantics=("arbitrary",), vmem_limit_bytes=_vmem_limit(vmem)),
        name="xattn",
    )(h, g, w_q, k, v, w_o)


def kernel(x, mem, g_ffn1, w_ffn1_in, w_ffn1_out, g_mix, w_mix_in, conv_w, conv_b, g_gm_v,
           w_spatial, b_spatial, w_mix_out, g_xattn, g_mem, w_xq, w_xk, w_xv, w_xo, g_ffn2,
           w_ffn2_in, w_ffn2_out, g_final):
    b, s, d = x.shape
    n_mem = mem.shape[1]
    depth = g_ffn1.shape[0]
    assert depth >= 1
    heads_per_step = MIX_COLS // GROUP_DIM
    ffn_tiles = dict(tm=2048, tf=256, row_chunk=1024)
    tm = 1024

    def row(v):
        return v.reshape(1, -1)

    h = x.reshape(b * s, d)
    mem2 = mem.reshape(b * n_mem, d)
    g_fin = row(g_final)
    for l in range(depth):
        last = l == depth - 1
        h = _ffn(h, row(g_ffn1[l]), w_ffn1_in[l], w_ffn1_out[l], g_fin,
                 final_norm=False, **ffn_tiles)
        b_s_t = b_spatial[l].reshape(-1, heads_per_step, CHUNK).transpose(0, 2, 1)
        h = _mix(h, row(g_mix[l]), w_mix_in[l], conv_w[l], row(conv_b[l]), row(g_gm_v[l]),
                 w_spatial[l], b_s_t, w_mix_out[l], seq=s, tm=tm)
        k, v = _kv(mem2, row(g_mem[l]), w_xk[l], w_xv[l], tn=512)
        h = _xattn(h, row(g_xattn[l]), w_xq[l], k, v, w_xo[l], seq=s, n_mem=n_mem, tm=tm)
        h = _ffn(h, row(g_ffn2[l]), w_ffn2_in[l], w_ffn2_out[l], g_fin,
                 final_norm=last, **ffn_tiles)
    return h.reshape(b, s, d)
```

```python
import functools

import jax
import jax.numpy as jnp
from jax import lax
from jax.experimental import pallas as pl
from jax.experimental.pallas import tpu as pltpu

GROUP_DIM = 128
CHUNK = 128
CONV_K = 3
XA_HEADS = 4
EPS = 1e-6

V7X_VMEM_BYTES = 64 * 1024 * 1024
SUBLANES = 8
MIX_COLS = 256
ROW_TILE_DMA_THREAD = 1

BF16 = jnp.bfloat16
F32 = jnp.float32


def _rmsnorm(x, g):
    y = x * lax.rsqrt(jnp.mean(x * x, axis=-1, keepdims=True) + EPS)
    return y * g


_dot = functools.partial(jnp.dot, preferred_element_type=F32)


TEMPORARIES_BYTES = 8 << 20
VMEM_RESERVE_BYTES = 4 << 20


def _vmem_limit(buffer_bytes):
    return min(buffer_bytes + buffer_bytes // 4 + TEMPORARIES_BYTES,
               V7X_VMEM_BYTES - VMEM_RESERVE_BYTES)


def _ffn_kernel(x_hbm, g_ref, win_hbm, wout_hbm, gf_ref, o_hbm, buf_ref, xn_ref, wg_ref, wu_ref,
                wo_ref, in_sem, out_sem, w_sem, *, final_norm, row_chunk):
    i = pl.program_id(0)
    n_tiles = pl.num_programs(0)
    tm, d = xn_ref.shape
    tf = wo_ref.shape[1]
    d_ff = wout_hbm.shape[0]
    nk = d_ff // tf
    slot = lax.rem(i, 2)
    other = 1 - slot

    def load(tile, dst):
        return pltpu.make_async_copy(x_hbm.at[pl.ds(tile * tm, tm)], buf_ref.at[dst], in_sem)

    def store(tile, src):
        return pltpu.make_async_copy(buf_ref.at[src], o_hbm.at[pl.ds(tile * tm, tm)], out_sem)

    def weight_copies(k, ws):
        cols = pl.ds(pl.multiple_of(k * tf, tf), tf)
        up_cols = pl.ds(pl.multiple_of(d_ff + k * tf, tf), tf)
        return (pltpu.make_async_copy(win_hbm.at[:, cols], wg_ref.at[ws], w_sem.at[0, ws]),
                pltpu.make_async_copy(win_hbm.at[:, up_cols], wu_ref.at[ws], w_sem.at[1, ws]),
                pltpu.make_async_copy(wout_hbm.at[cols, :], wo_ref.at[ws], w_sem.at[2, ws]))

    @pl.when(i == 0)
    def _():
        load(0, 0).start(priority=ROW_TILE_DMA_THREAD)
        for c in weight_copies(0, 0):
            c.start()

    load(i, slot).wait()
    xn_ref[...] = _rmsnorm(buf_ref[slot], g_ref[...]).astype(BF16)

    @pl.when(i > 0)
    def _():
        store(i - 1, other).start(priority=ROW_TILE_DMA_THREAD)

    def block(k, carry):
        @pl.when(k == nk // 2)
        def _():
            @pl.when(i > 0)
            def _():
                store(i - 1, other).wait()

            @pl.when(i + 1 < n_tiles)
            def _():
                load(i + 1, other).start(priority=ROW_TILE_DMA_THREAD)

        ws = lax.rem(i * nk + k, 2)
        for c in weight_copies(k, ws):
            c.wait()

        for c in weight_copies(lax.rem(k + 1, nk), 1 - ws):
            c.start()

        w_gate, w_up, w_down = (r[ws].astype(BF16) for r in (wg_ref, wu_ref, wo_ref))
        for rows in (pl.ds(r * row_chunk, row_chunk) for r in range(tm // row_chunk)):
            xn = xn_ref[rows, :]
            gate = _dot(xn, w_gate)
            up = _dot(xn, w_up)
            hidden = (0.5 * gate * jax.nn.sigmoid(gate) * up).astype(BF16)
            buf_ref[slot, rows, :] += _dot(hidden, w_down)
        return carry

    lax.fori_loop(0, nk, block, 0)

    if final_norm:
        buf_ref[slot] = _rmsnorm(buf_ref[slot], gf_ref[...])

    @pl.when(i == n_tiles - 1)
    def _():
        store(i, slot).start(priority=ROW_TILE_DMA_THREAD)
        for c in weight_copies(0, lax.rem(n_tiles * nk, 2)):
            c.wait()
        store(i, slot).wait()


def _ffn(x, g, w_in, w_out, g_final, *, final_norm, tm, tf, row_chunk):
    m, d = x.shape
    d_ff = w_out.shape[0]
    nk = d_ff // tf
    assert m % tm == 0 and d_ff % tf == 0 and nk >= 2 and tm % row_chunk == 0
    vmem = (2 * tm * d * 4
            + tm * d * 2
            + 2 * 3 * d * tf * w_in.dtype.itemsize)
    return pl.pallas_call(
        functools.partial(_ffn_kernel, final_norm=final_norm, row_chunk=row_chunk),
        grid=(m // tm,),
        in_specs=[
            pl.BlockSpec(memory_space=pl.ANY),
            pl.BlockSpec((1, d), lambda i: (0, 0)),
            pl.BlockSpec(memory_space=pl.ANY),
            pl.BlockSpec(memory_space=pl.ANY),
            pl.BlockSpec((1, d), lambda i: (0, 0)),
        ],
        out_specs=pl.BlockSpec(memory_space=pl.ANY),
        out_shape=jax.ShapeDtypeStruct((m, d), F32),
        scratch_shapes=[pltpu.VMEM((2, tm, d), F32), pltpu.VMEM((tm, d), BF16),
                        pltpu.VMEM((2, d, tf), w_in.dtype), pltpu.VMEM((2, d, tf), w_in.dtype),
                        pltpu.VMEM((2, tf, d), w_out.dtype),
                        pltpu.SemaphoreType.DMA(()), pltpu.SemaphoreType.DMA(()),
                        pltpu.SemaphoreType.DMA((3, 2))],
        compiler_params=pltpu.CompilerParams(
            dimension_semantics=("arbitrary",), vmem_limit_bytes=_vmem_limit(vmem)),
        name="ffn_final" if final_norm else "ffn",
    )(x, g, w_in, w_out, g_final)


def _stream_row_tiles(s, per, n_tiles, h_hbm, o_hbm, buf_ref, in_sem, out_sem):
    tm = buf_ref.shape[1]
    t = s // per
    j = lax.rem(s, per)

    def load(tile):
        return pltpu.make_async_copy(h_hbm.at[pl.ds(tile * tm, tm)],
                                     buf_ref.at[lax.rem(tile, 2)], in_sem)

    def store(tile):
        return pltpu.make_async_copy(buf_ref.at[lax.rem(tile, 2)],
                                     o_hbm.at[pl.ds(tile * tm, tm)], out_sem)

    @pl.when(s == 0)
    def _():
        load(0).start(priority=ROW_TILE_DMA_THREAD)

    @pl.when((j == 0) & (t < n_tiles))
    def _():
        load(t).wait()

    @pl.when((j == 1) & (t >= 1))
    def _():
        store(t - 1).start(priority=ROW_TILE_DMA_THREAD)

    @pl.when(j == 2)
    def _():
        @pl.when(t >= 1)
        def _():
            store(t - 1).wait()

        @pl.when(t + 1 < n_tiles)
        def _():
            load(t + 1).start(priority=ROW_TILE_DMA_THREAD)

    return store


def _mix_kernel(h_hbm, g_ref, wb_ref, wc_ref, wh_ref, wu_ref, wv_ref, cw_ref, cb_ref,
                gv_ref, ws_ref, bs_ref, woc_ref, wog_ref, o_hbm, buf_ref, hn_ref, y_ref, carry_ref,
                in_sem, out_sem, *, nj, n_tiles, tiles_per_seq):
    s = pl.program_id(0)
    j = lax.rem(s, nj)
    cur = lax.rem(s, 2)
    prev = 1 - cur
    tm = hn_ref.shape[0]
    tile_slot = lax.rem(s // nj, 2)
    prev_slot = lax.rem(jnp.maximum(s - 1, 0) // nj, 2)
    store = _stream_row_tiles(s, nj, n_tiles, h_hbm, o_hbm, buf_ref, in_sem, out_sem)

    @pl.when(s == 0)
    def _():
        y_ref[prev] = jnp.zeros(y_ref.shape[1:], BF16)

    @pl.when((j == 0) & (s < n_tiles * nj))
    def _():
        hn_ref[...] = _rmsnorm(buf_ref[tile_slot], g_ref[...]).astype(BF16)

    @pl.when(lax.rem(s // nj, tiles_per_seq) == 0)
    def _():
        carry_ref[j] = jnp.zeros((SUBLANES, MIX_COLS), F32)

    def step(compute_block):
        if compute_block:
            w_in = jnp.concatenate(
                [r[...].astype(BF16) for r in (wv_ref, wu_ref, wc_ref, wh_ref, wb_ref)], axis=1)
            proj = _dot(hn_ref[...], w_in)
            v, u, gate_c, h_c, gate_b = (proj[:, n * MIX_COLS:(n + 1) * MIX_COLS]
                                         for n in range(5))

        y_prev = y_ref[prev]
        buf_ref[prev_slot] += _dot(y_prev[:, :MIX_COLS], woc_ref[...].astype(BF16))

        if compute_block:
            gv = gv_ref[...]
            bs = bs_ref[...]
            tri = (lax.broadcasted_iota(jnp.int32, (CHUNK, CHUNK), 0)
                   >= lax.broadcasted_iota(jnp.int32, (CHUNK, CHUNK), 1))
            sgs = []
            for hh in range(MIX_COLS // GROUP_DIM):
                lanes = slice(hh * GROUP_DIM, (hh + 1) * GROUP_DIM)
                vh = _rmsnorm(v[:, lanes], gv[:, lanes]).astype(BF16)
                w = jnp.where(tri, ws_ref[hh], 0.0).astype(BF16)
                sgs.append([_dot(w, vh[c * CHUNK:(c + 1) * CHUNK]) for c in range(tm // CHUNK)])

        buf_ref[prev_slot] += _dot(y_prev[:, MIX_COLS:], wog_ref[...].astype(BF16))

        if not compute_block:
            return
        for hh, per_chunk in enumerate(sgs):
            lanes = slice(hh * GROUP_DIM, (hh + 1) * GROUP_DIM)
            for c, sg in enumerate(per_chunk):
                rows = slice(c * CHUNK, (c + 1) * CHUNK)
                y_ref[cur, rows, MIX_COLS + hh * GROUP_DIM:MIX_COLS + (hh + 1) * GROUP_DIM] = (
                    u[rows, lanes] * (sg + bs[:, hh:hh + 1])).astype(BF16)

        z = gate_c * h_c
        tail = carry_ref[j]
        carry_ref[j] = z[tm - SUBLANES:, :]
        cw = cw_ref[...]
        cb = cb_ref[...]
        row = lax.broadcasted_iota(jnp.int32, (SUBLANES, MIX_COLS), 0)
        z1 = pltpu.roll(z, 1, 0)
        z2 = pltpu.roll(z, 2, 0)
        head1 = jnp.where(row < 1, pltpu.roll(tail, 1, 0), z1[:SUBLANES])
        head2 = jnp.where(row < 2, pltpu.roll(tail, 2, 0), z2[:SUBLANES])
        z1 = jnp.concatenate([head1, z1[SUBLANES:]], axis=0)
        z2 = jnp.concatenate([head2, z2[SUBLANES:]], axis=0)
        y_ref[cur, :, :MIX_COLS] = (
            gate_b * (cb + cw[2:3] * z + cw[1:2] * z1 + cw[0:1] * z2)).astype(BF16)

    @pl.when(s < n_tiles * nj)
    def _():
        step(compute_block=True)

    @pl.when(s == n_tiles * nj)
    def _():
        step(compute_block=False)
        store(n_tiles - 1).start(priority=ROW_TILE_DMA_THREAD)
        store(n_tiles - 1).wait()


def _mix(h, g, w_in, conv_w, conv_b, g_v, w_s, b_s_t, w_out, *, seq, tm):
    m, d = h.shape
    width = w_out.shape[0]
    conv_width = width // 2
    nj = conv_width // MIX_COLS
    heads_per_step = MIX_COLS // GROUP_DIM
    n_tiles = m // tm
    n_steps = n_tiles * nj
    assert m % tm == 0 and seq % tm == 0 and tm % CHUNK == 0 and nj >= 3

    def prev_j(s):
        return jnp.maximum(s - 1, 0) % nj

    def col_block(offset):
        return pl.BlockSpec((d, MIX_COLS), lambda s: (0, offset + s % nj))

    vmem = (2 * tm * d * 4 + tm * d * 2 + 2 * tm * 2 * MIX_COLS * 2
            + 2 * 5 * d * MIX_COLS * w_in.dtype.itemsize
            + 2 * 2 * MIX_COLS * d * w_out.dtype.itemsize)
    return pl.pallas_call(
        functools.partial(_mix_kernel, nj=nj, n_tiles=n_tiles, tiles_per_seq=seq // tm),
        grid=(n_steps + 1,),
        in_specs=[
            pl.BlockSpec(memory_space=pl.ANY),
            pl.BlockSpec((1, d), lambda s: (0, 0)),
            col_block(0), col_block(nj), col_block(2 * nj),
            col_block(3 * nj), col_block(4 * nj),
            pl.BlockSpec((CONV_K, MIX_COLS), lambda s: (0, s % nj)),
            pl.BlockSpec((1, MIX_COLS), lambda s: (0, s % nj)),
            pl.BlockSpec((1, MIX_COLS), lambda s: (0, s % nj)),
            pl.BlockSpec((heads_per_step, CHUNK, CHUNK), lambda s: (s % nj, 0, 0)),
            pl.BlockSpec((None, CHUNK, heads_per_step), lambda s: (s % nj, 0, 0)),
            pl.BlockSpec((MIX_COLS, d), lambda s: (prev_j(s), 0)),
            pl.BlockSpec((MIX_COLS, d), lambda s: (nj + prev_j(s), 0)),
        ],
        out_specs=pl.BlockSpec(memory_space=pl.ANY),
        out_shape=jax.ShapeDtypeStruct((m, d), F32),
        scratch_shapes=[pltpu.VMEM((2, tm, d), F32), pltpu.VMEM((tm, d), BF16),
                        pltpu.VMEM((2, tm, 2 * MIX_COLS), BF16),
                        pltpu.VMEM((nj, SUBLANES, MIX_COLS), F32),
                        pltpu.SemaphoreType.DMA(()), pltpu.SemaphoreType.DMA(())],
        compiler_params=pltpu.CompilerParams(
            dimension_semantics=("arbitrary",), vmem_limit_bytes=_vmem_limit(vmem)),
        name="mix",
    )(h, g, w_in, w_in, w_in, w_in, w_in, conv_w, conv_b, g_v, w_s, b_s_t, w_out, w_out)


def _kv_kernel(mem_ref, g_ref, wk_ref, wv_ref, k_ref, v_ref, mn_ref):
    @pl.when(pl.program_id(0) == 0)
    def _():
        mn_ref[...] = _rmsnorm(mem_ref[...], g_ref[...]).astype(BF16)

    mn = mn_ref[...]
    k_ref[...] = _dot(mn, wk_ref[...].astype(BF16)).astype(BF16)
    v_ref[...] = _dot(mn, wv_ref[...].astype(BF16)).astype(BF16)


def _kv(mem, g, w_k, w_v, *, tn):
    m, d = mem.shape
    vmem = (2 * m * d * 4 + m * d * 2 + 2 * 2 * d * tn * w_k.dtype.itemsize
            + 2 * 2 * m * tn * 2)
    return pl.pallas_call(
        _kv_kernel,
        grid=(d // tn,),
        in_specs=[
            pl.BlockSpec((m, d), lambda n: (0, 0)),
            pl.BlockSpec((1, d), lambda n: (0, 0)),
            pl.BlockSpec((d, tn), lambda n: (0, n)),
            pl.BlockSpec((d, tn), lambda n: (0, n)),
        ],
        out_specs=[pl.BlockSpec((m, tn), lambda n: (0, n))] * 2,
        out_shape=[jax.ShapeDtypeStruct((m, d), BF16)] * 2,
        scratch_shapes=[pltpu.VMEM((m, d), BF16)],
        compiler_params=pltpu.CompilerParams(
            dimension_semantics=("arbitrary",), vmem_limit_bytes=_vmem_limit(vmem)),
        name="kv",
    )(mem, g, w_k, w_v)


def _xattn_kernel(h_hbm, g_ref, wq_ref, k_ref, v_ref, wo_ref, o_hbm, buf_ref, hn_ref, q_ref,
                  wq_all, wo_all, in_sem, out_sem, *, n_tiles):
    s = pl.program_id(0)
    cur = lax.rem(s, 2)
    prev = 1 - cur
    head = lax.rem(s, XA_HEADS)
    prev_head = lax.rem(jnp.maximum(s - 1, 0), XA_HEADS)
    tile_slot = lax.rem(s // XA_HEADS, 2)
    prev_slot = lax.rem(jnp.maximum(s - 1, 0) // XA_HEADS, 2)
    store = _stream_row_tiles(s, XA_HEADS, n_tiles, h_hbm, o_hbm, buf_ref, in_sem, out_sem)

    @pl.when(s == 0)
    def _():
        q_ref[prev] = jnp.zeros(q_ref.shape[1:], BF16)

    @pl.when(s < XA_HEADS)
    def _():
        wq_all[head] = wq_ref[...].astype(BF16)

    @pl.when(s <= XA_HEADS)
    def _():
        wo_all[prev_head] = wo_ref[...].astype(BF16)

    @pl.when((lax.rem(s, XA_HEADS) == 0) & (s < n_tiles * XA_HEADS))
    def _():
        hn_ref[...] = _rmsnorm(buf_ref[tile_slot], g_ref[...]).astype(BF16)

    def step(project):
        q_prev = q_ref[prev]
        scores = lax.dot_general(q_prev, k_ref[...], (((1,), (1,)), ((), ())),
                                 preferred_element_type=F32)
        if project:
            q_ref[cur] = _dot(hn_ref[...], wq_all[head]).astype(BF16)
        scores = scores * (q_prev.shape[-1] ** -0.5)
        p = jnp.exp(scores - jnp.max(scores, axis=-1, keepdims=True))
        p = p / jnp.sum(p, axis=-1, keepdims=True)
        o = _dot(p.astype(BF16), v_ref[...])
        o = jnp.where(s > 0, o, 0.0)
        buf_ref[prev_slot] += _dot(o.astype(BF16), wo_all[prev_head])

    @pl.when(s < n_tiles * XA_HEADS)
    def _():
        step(project=True)

    @pl.when(s == n_tiles * XA_HEADS)
    def _():
        step(project=False)
        store(n_tiles - 1).start(priority=ROW_TILE_DMA_THREAD)
        store(n_tiles - 1).wait()


def _xattn(h, g, w_q, k, v, w_o, *, seq, n_mem, tm):
    m, d = h.shape
    hdim = d // XA_HEADS
    tiles_per_seq = seq // tm
    n_tiles = m // tm
    n_steps = n_tiles * XA_HEADS
    assert m % tm == 0 and seq % tm == 0 and XA_HEADS >= 3

    def prev_tile(s):
        return jnp.maximum(s - 1, 0) // XA_HEADS

    def prev_head(s):
        return jnp.maximum(s - 1, 0) % XA_HEADS

    vmem = (2 * tm * d * 4 + tm * d * 2 + 2 * tm * hdim * 2
            + 2 * 2 * d * hdim * w_q.dtype.itemsize + 2 * 2 * n_mem * hdim * 2
            + 2 * d * d * 2)
    return pl.pallas_call(
        functools.partial(_xattn_kernel, n_tiles=n_tiles),
        grid=(n_steps + 1,),
        in_specs=[
            pl.BlockSpec(memory_space=pl.ANY),
            pl.BlockSpec((1, d), lambda s: (0, 0)),
            pl.BlockSpec((d, hdim), lambda s: (0, jnp.minimum(s, XA_HEADS - 1))),
            pl.BlockSpec((n_mem, hdim), lambda s: (prev_tile(s) // tiles_per_seq, prev_head(s))),
            pl.BlockSpec((n_mem, hdim), lambda s: (prev_tile(s) // tiles_per_seq, prev_head(s))),
            pl.BlockSpec((hdim, d), lambda s: (prev_head(jnp.minimum(s, XA_HEADS)), 0)),
        ],
        out_specs=pl.BlockSpec(memory_space=pl.ANY),
        out_shape=jax.ShapeDtypeStruct((m, d), F32),
        scratch_shapes=[pltpu.VMEM((2, tm, d), F32), pltpu.VMEM((tm, d), BF16),
                        pltpu.VMEM((2, tm, hdim), BF16),
                        pltpu.VMEM((XA_HEADS, d, hdim), BF16), pltpu.VMEM((XA_HEADS, hdim, d), BF16),
                        pltpu.SemaphoreType.DMA(()), pltpu.SemaphoreType.DMA(())],
        compiler_params=pltpu.CompilerParams(
            dimension_semantics=("arbitrary",), vmem_limit_bytes=_vmem_limit(vmem)),
        name="xattn",
    )(h, g, w_q, k, v, w_o)


def kernel(x, mem, g_ffn1, w_ffn1_in, w_ffn1_out, g_mix, w_mix_in, conv_w, conv_b, g_gm_v,
           w_spatial, b_spatial, w_mix_out, g_xattn, g_mem, w_xq, w_xk, w_xv, w_xo, g_ffn2,
           w_ffn2_in, w_ffn2_out, g_final):
    b, s, d = x.shape
    n_mem = mem.shape[1]
    depth = g_ffn1.shape[0]
    assert depth >= 1
    heads_per_step = MIX_COLS // GROUP_DIM
    ffn_tiles = dict(tm=2048, tf=256, row_chunk=1024)
    tm = 1024

    def row(v):
        return v.reshape(1, -1)

    h = x.reshape(b * s, d)
    mem2 = mem.reshape(b * n_mem, d)
    g_fin = row(g_final)
    for l in range(depth):
        last = l == depth - 1
        h = _ffn(h, row(g_ffn1[l]), w_ffn1_in[l], w_ffn1_out[l], g_fin,
                 final_norm=False, **ffn_tiles)
        b_s_t = b_spatial[l].reshape(-1, heads_per_step, CHUNK).transpose(0, 2, 1)
        h = _mix(h, row(g_mix[l]), w_mix_in[l], conv_w[l], row(conv_b[l]), row(g_gm_v[l]),
                 w_spatial[l], b_s_t, w_mix_out[l], seq=s, tm=tm)
        k, v = _kv(mem2, row(g_mem[l]), w_xk[l], w_xv[l], tn=512)
        h = _xattn(h, row(g_xattn[l]), w_xq[l], k, v, w_xo[l], seq=s, n_mem=n_mem, tm=tm)
        h = _ffn(h, row(g_ffn2[l]), w_ffn2_in[l], w_ffn2_out[l], g_fin,
                 final_norm=last, **ffn_tiles)
    return h.reshape(b, s, d)
```

```python
import functools

import jax
import jax.numpy as jnp
from jax import lax
from jax.experimental import pallas as pl
from jax.experimental.pallas import tpu as pltpu

GROUP_DIM = 128
CHUNK = 128
CONV_K = 3
XA_HEADS = 4
EPS = 1e-6

V7X_VMEM_BYTES = 64 * 1024 * 1024
SUBLANES = 8
MIX_COLS = 256
ROW_TILE_DMA_THREAD = 1

BF16 = jnp.bfloat16
F32 = jnp.float32


def _rmsnorm(x, g):
    y = x * lax.rsqrt(jnp.mean(x * x, axis=-1, keepdims=True) + EPS)
    return y * g


_dot = functools.partial(jnp.dot, preferred_element_type=F32)


TEMPORARIES_BYTES = 8 << 20
VMEM_RESERVE_BYTES = 4 << 20


def _vmem_limit(buffer_bytes):
    return min(buffer_bytes + buffer_bytes // 4 + TEMPORARIES_BYTES,
               V7X_VMEM_BYTES - VMEM_RESERVE_BYTES)


def _ffn_kernel(x_hbm, g_ref, win_hbm, wout_hbm, gf_ref, o_hbm, buf_ref, xn_ref, wg_ref, wu_ref,
                wo_ref, in_sem, out_sem, w_sem, *, final_norm, row_chunk):
    i = pl.program_id(0)
    n_tiles = pl.num_programs(0)
    tm, d = xn_ref.shape
    tf = wo_ref.shape[1]
    d_ff = wout_hbm.shape[0]
    nk = d_ff // tf
    slot = lax.rem(i, 2)
    other = 1 - slot

    def load(tile, dst):
        return pltpu.make_async_copy(x_hbm.at[pl.ds(tile * tm, tm)], buf_ref.at[dst], in_sem)

    def store(tile, src):
        return pltpu.make_async_copy(buf_ref.at[src], o_hbm.at[pl.ds(tile * tm, tm)], out_sem)

    def weight_copies(k, ws):
        cols = pl.ds(pl.multiple_of(k * tf, tf), tf)
        up_cols = pl.ds(pl.multiple_of(d_ff + k * tf, tf), tf)
        return (pltpu.make_async_copy(win_hbm.at[:, cols], wg_ref.at[ws], w_sem.at[0, ws]),
                pltpu.make_async_copy(win_hbm.at[:, up_cols], wu_ref.at[ws], w_sem.at[1, ws]),
                pltpu.make_async_copy(wout_hbm.at[cols, :], wo_ref.at[ws], w_sem.at[2, ws]))

    @pl.when(i == 0)
    def _():
        load(0, 0).start(priority=ROW_TILE_DMA_THREAD)
        for c in weight_copies(0, 0):
            c.start()

    load(i, slot).wait()
    xn_ref[...] = _rmsnorm(buf_ref[slot], g_ref[...]).astype(BF16)

    @pl.when(i > 0)
    def _():
        store(i - 1, other).start(priority=ROW_TILE_DMA_THREAD)

    def block(k, carry):
        @pl.when(k == nk // 2)
        def _():
            @pl.when(i > 0)
            def _():
                store(i - 1, other).wait()

            @pl.when(i + 1 < n_tiles)
            def _():
                load(i + 1, other).start(priority=ROW_TILE_DMA_THREAD)

        ws = lax.rem(i * nk + k, 2)
        for c in weight_copies(k, ws):
            c.wait()

        for c in weight_copies(lax.rem(k + 1, nk), 1 - ws):
            c.start()

        w_gate, w_up, w_down = (r[ws].astype(BF16) for r in (wg_ref, wu_ref, wo_ref))
        for rows in (pl.ds(r * row_chunk, row_chunk) for r in range(tm // row_chunk)):
            xn = xn_ref[rows, :]
            gate = _dot(xn, w_gate)
            up = _dot(xn, w_up)
            hidden = (0.5 * gate * jax.nn.sigmoid(gate) * up).astype(BF16)
            buf_ref[slot, rows, :] += _dot(hidden, w_down)
        return carry

    lax.fori_loop(0, nk, block, 0)

    if final_norm:
        buf_ref[slot] = _rmsnorm(buf_ref[slot], gf_ref[...])

    @pl.when(i == n_tiles - 1)
    def _():
        store(i, slot).start(priority=ROW_TILE_DMA_THREAD)
        for c in weight_copies(0, lax.rem(n_tiles * nk, 2)):
            c.wait()
        store(i, slot).wait()


def _ffn(x, g, w_in, w_out, g_final, *, final_norm, tm, tf, row_chunk):
    m, d = x.shape
    d_ff = w_out.shape[0]
    nk = d_ff // tf
    assert m % tm == 0 and d_ff % tf == 0 and nk >= 2 and tm % row_chunk == 0
    vmem = (2 * tm * d * 4
            + tm * d * 2
            + 2 * 3 * d * tf * w_in.dtype.itemsize)
    return pl.pallas_call(
        functools.partial(_ffn_kernel, final_norm=final_norm, row_chunk=row_chunk),
        grid=(m // tm,),
        in_specs=[
            pl.BlockSpec(memory_space=pl.ANY),
            pl.BlockSpec((1, d), lambda i: (0, 0)),
            pl.BlockSpec(memory_space=pl.ANY),
            pl.BlockSpec(memory_space=pl.ANY),
            pl.BlockSpec((1, d), lambda i: (0, 0)),
        ],
        out_specs=pl.BlockSpec(memory_space=pl.ANY),
        out_shape=jax.ShapeDtypeStruct((m, d), F32),
        scratch_shapes=[pltpu.VMEM((2, tm, d), F32), pltpu.VMEM((tm, d), BF16),
                        pltpu.VMEM((2, d, tf), w_in.dtype), pltpu.VMEM((2, d, tf), w_in.dtype),
                        pltpu.VMEM((2, tf, d), w_out.dtype),
                        pltpu.SemaphoreType.DMA(()), pltpu.SemaphoreType.DMA(()),
                        pltpu.SemaphoreType.DMA((3, 2))],
        compiler_params=pltpu.CompilerParams(
            dimension_semantics=("arbitrary",), vmem_limit_bytes=_vmem_limit(vmem)),
        name="ffn_final" if final_norm else "ffn",
    )(x, g, w_in, w_out, g_final)


def _stream_row_tiles(s, per, n_tiles, h_hbm, o_hbm, buf_ref, in_sem, out_sem):
    tm = buf_ref.shape[1]
    t = s // per
    j = lax.rem(s, per)

    def load(tile):
        return pltpu.make_async_copy(h_hbm.at[pl.ds(tile * tm, tm)],
                                     buf_ref.at[lax.rem(tile, 2)], in_sem)

    def store(tile):
        return pltpu.make_async_copy(buf_ref.at[lax.rem(tile, 2)],
                                     o_hbm.at[pl.ds(tile * tm, tm)], out_sem)

    @pl.when(s == 0)
    def _():
        load(0).start(priority=ROW_TILE_DMA_THREAD)

    @pl.when((j == 0) & (t < n_tiles))
    def _():
        load(t).wait()

    @pl.when((j == 1) & (t >= 1))
    def _():
        store(t - 1).start(priority=ROW_TILE_DMA_THREAD)

    @pl.when(j == 2)
    def _():
        @pl.when(t >= 1)
        def _():
            store(t - 1).wait()

        @pl.when(t + 1 < n_tiles)
        def _():
            load(t + 1).start(priority=ROW_TILE_DMA_THREAD)

    return store


def _mix_kernel(h_hbm, g_ref, wb_ref, wc_ref, wh_ref, wu_ref, wv_ref, cw_ref, cb_ref,
                gv_ref, ws_ref, bs_ref, woc_ref, wog_ref, o_hbm, buf_ref, hn_ref, y_ref, carry_ref,
                in_sem, out_sem, *, nj, n_tiles, tiles_per_seq):
    s = pl.program_id(0)
    j = lax.rem(s, nj)
    cur = lax.rem(s, 2)
    prev = 1 - cur
    tm = hn_ref.shape[0]
    tile_slot = lax.rem(s // nj, 2)
    prev_slot = lax.rem(jnp.maximum(s - 1, 0) // nj, 2)
    store = _stream_row_tiles(s, nj, n_tiles, h_hbm, o_hbm, buf_ref, in_sem, out_sem)

    @pl.when(s == 0)
    def _():
        y_ref[prev] = jnp.zeros(y_ref.shape[1:], BF16)

    @pl.when((j == 0) & (s < n_tiles * nj))
    def _():
        hn_ref[...] = _rmsnorm(buf_ref[tile_slot], g_ref[...]).astype(BF16)

    @pl.when(lax.rem(s // nj, tiles_per_seq) == 0)
    def _():
        carry_ref[j] = jnp.zeros((SUBLANES, MIX_COLS), F32)

    def step(compute_block):
        if compute_block:
            w_in = jnp.concatenate(
                [r[...].astype(BF16) for r in (wv_ref, wu_ref, wc_ref, wh_ref, wb_ref)], axis=1)
            proj = _dot(hn_ref[...], w_in)
            v, u, gate_c, h_c, gate_b = (proj[:, n * MIX_COLS:(n + 1) * MIX_COLS]
                                         for n in range(5))

        y_prev = y_ref[prev]
        buf_ref[prev_slot] += _dot(y_prev[:, :MIX_COLS], woc_ref[...].astype(BF16))

        if compute_block:
            gv = gv_ref[...]
            bs = bs_ref[...]
            tri = (lax.broadcasted_iota(jnp.int32, (CHUNK, CHUNK), 0)
                   >= lax.broadcasted_iota(jnp.int32, (CHUNK, CHUNK), 1))
            sgs = []
            for hh in range(MIX_COLS // GROUP_DIM):
                lanes = slice(hh * GROUP_DIM, (hh + 1) * GROUP_DIM)
                vh = _rmsnorm(v[:, lanes], gv[:, lanes]).astype(BF16)
                w = jnp.where(tri, ws_ref[hh], 0.0).astype(BF16)
                sgs.append([_dot(w, vh[c * CHUNK:(c + 1) * CHUNK]) for c in range(tm // CHUNK)])

        buf_ref[prev_slot] += _dot(y_prev[:, MIX_COLS:], wog_ref[...].astype(BF16))

        if not compute_block:
            return
        for hh, per_chunk in enumerate(sgs):
            lanes = slice(hh * GROUP_DIM, (hh + 1) * GROUP_DIM)
            for c, sg in enumerate(per_chunk):
                rows = slice(c * CHUNK, (c + 1) * CHUNK)
                y_ref[cur, rows, MIX_COLS + hh * GROUP_DIM:MIX_COLS + (hh + 1) * GROUP_DIM] = (
                    u[rows, lanes] * (sg + bs[:, hh:hh + 1])).astype(BF16)

        z = gate_c * h_c
        tail = carry_ref[j]
        carry_ref[j] = z[tm - SUBLANES:, :]
        cw = cw_ref[...]
        cb = cb_ref[...]
        row = lax.broadcasted_iota(jnp.int32, (SUBLANES, MIX_COLS), 0)
        z1 = pltpu.roll(z, 1, 0)
        z2 = pltpu.roll(z, 2, 0)
        head1 = jnp.where(row < 1, pltpu.roll(tail, 1, 0), z1[:SUBLANES])
        head2 = jnp.where(row < 2, pltpu.roll(tail, 2, 0), z2[:SUBLANES])
        z1 = jnp.concatenate([head1, z1[SUBLANES:]], axis=0)
        z2 = jnp.concatenate([head2, z2[SUBLANES:]], axis=0)
        y_ref[cur, :, :MIX_COLS] = (
            gate_b * (cb + cw[2:3] * z + cw[1:2] * z1 + cw[0:1] * z2)).astype(BF16)

    @pl.when(s < n_tiles * nj)
    def _():
        step(compute_block=True)

    @pl.when(s == n_tiles * nj)
    def _():
        step(compute_block=False)
        store(n_tiles - 1).start(priority=ROW_TILE_DMA_THREAD)
        store(n_tiles - 1).wait()


def _mix(h, g, w_in, conv_w, conv_b, g_v, w_s, b_s_t, w_out, *, seq, tm):
    m, d = h.shape
    width = w_out.shape[0]
    conv_width = width // 2
    nj = conv_width // MIX_COLS
    heads_per_step = MIX_COLS // GROUP_DIM
    n_tiles = m // tm
    n_steps = n_tiles * nj
    assert m % tm == 0 and seq % tm == 0 and tm % CHUNK == 0 and nj >= 3

    def prev_j(s):
        return jnp.maximum(s - 1, 0) % nj

    def col_block(offset):
        return pl.BlockSpec((d, MIX_COLS), lambda s: (0, offset + s % nj))

    vmem = (2 * tm * d * 4 + tm * d * 2 + 2 * tm * 2 * MIX_COLS * 2
            + 2 * 5 * d * MIX_COLS * w_in.dtype.itemsize
            + 2 * 2 * MIX_COLS * d * w_out.dtype.itemsize)
    return pl.pallas_call(
        functools.partial(_mix_kernel, nj=nj, n_tiles=n_tiles, tiles_per_seq=seq // tm),
        grid=(n_steps + 1,),
        in_specs=[
            pl.BlockSpec(memory_space=pl.ANY),
            pl.BlockSpec((1, d), lambda s: (0, 0)),
            col_block(0), col_block(nj), col_block(2 * nj),
            col_block(3 * nj), col_block(4 * nj),
            pl.BlockSpec((CONV_K, MIX_COLS), lambda s: (0, s % nj)),
            pl.BlockSpec((1, MIX_COLS), lambda s: (0, s % nj)),
            pl.BlockSpec((1, MIX_COLS), lambda s: (0, s % nj)),
            pl.BlockSpec((heads_per_step, CHUNK, CHUNK), lambda s: (s % nj, 0, 0)),
            pl.BlockSpec((None, CHUNK, heads_per_step), lambda s: (s % nj, 0, 0)),
            pl.BlockSpec((MIX_COLS, d), lambda s: (prev_j(s), 0)),
            pl.BlockSpec((MIX_COLS, d), lambda s: (nj + prev_j(s), 0)),
        ],
        out_specs=pl.BlockSpec(memory_space=pl.ANY),
        out_shape=jax.ShapeDtypeStruct((m, d), F32),
        scratch_shapes=[pltpu.VMEM((2, tm, d), F32), pltpu.VMEM((tm, d), BF16),
                        pltpu.VMEM((2, tm, 2 * MIX_COLS), BF16),
                        pltpu.VMEM((nj, SUBLANES, MIX_COLS), F32),
                        pltpu.SemaphoreType.DMA(()), pltpu.SemaphoreType.DMA(())],
        compiler_params=pltpu.CompilerParams(
            dimension_semantics=("arbitrary",), vmem_limit_bytes=_vmem_limit(vmem)),
        name="mix",
    )(h, g, w_in, w_in, w_in, w_in, w_in, conv_w, conv_b, g_v, w_s, b_s_t, w_out, w_out)


def _kv_kernel(mem_ref, g_ref, wk_ref, wv_ref, k_ref, v_ref, mn_ref):
    @pl.when(pl.program_id(0) == 0)
    def _():
        mn_ref[...] = _rmsnorm(mem_ref[...], g_ref[...]).astype(BF16)

    mn = mn_ref[...]
    k_ref[...] = _dot(mn, wk_ref[...].astype(BF16)).T.astype(BF16)
    v_ref[...] = _dot(mn, wv_ref[...].astype(BF16)).astype(BF16)


def _kv(mem, g, w_k, w_v, *, tn):
    m, d = mem.shape
    vmem = (2 * m * d * 4 + m * d * 2 + 2 * 2 * d * tn * w_k.dtype.itemsize
            + 2 * 2 * m * tn * 2)
    return pl.pallas_call(
        _kv_kernel,
        grid=(d // tn,),
        in_specs=[
            pl.BlockSpec((m, d), lambda n: (0, 0)),
            pl.BlockSpec((1, d), lambda n: (0, 0)),
            pl.BlockSpec((d, tn), lambda n: (0, n)),
            pl.BlockSpec((d, tn), lambda n: (0, n)),
        ],
        out_specs=[pl.BlockSpec((tn, m), lambda n: (n, 0)), pl.BlockSpec((m, tn), lambda n: (0, n))],
        out_shape=[jax.ShapeDtypeStruct((d, m), BF16), jax.ShapeDtypeStruct((m, d), BF16)],
        scratch_shapes=[pltpu.VMEM((m, d), BF16)],
        compiler_params=pltpu.CompilerParams(
            dimension_semantics=("arbitrary",), vmem_limit_bytes=_vmem_limit(vmem)),
        name="kv",
    )(mem, g, w_k, w_v)


def _xattn_kernel(h_hbm, g_ref, wq_ref, k_ref, v_ref, wo_ref, o_hbm, buf_ref, hn_ref, q_ref,
                  wq_all, wo_all, in_sem, out_sem, *, n_tiles):
    s = pl.program_id(0)
    cur = lax.rem(s, 2)
    prev = 1 - cur
    head = lax.rem(s, XA_HEADS)
    prev_head = lax.rem(jnp.maximum(s - 1, 0), XA_HEADS)
    tile_slot = lax.rem(s // XA_HEADS, 2)
    prev_slot = lax.rem(jnp.maximum(s - 1, 0) // XA_HEADS, 2)
    store = _stream_row_tiles(s, XA_HEADS, n_tiles, h_hbm, o_hbm, buf_ref, in_sem, out_sem)

    @pl.when(s == 0)
    def _():
        q_ref[prev] = jnp.zeros(q_ref.shape[1:], BF16)

    @pl.when(s < XA_HEADS)
    def _():
        wq_all[head] = wq_ref[...].astype(BF16)

    @pl.when(s <= XA_HEADS)
    def _():
        wo_all[prev_head] = wo_ref[...].astype(BF16)

    @pl.when((lax.rem(s, XA_HEADS) == 0) & (s < n_tiles * XA_HEADS))
    def _():
        hn_ref[...] = _rmsnorm(buf_ref[tile_slot], g_ref[...]).astype(BF16)

    def step(project):
        q_prev = q_ref[prev]
        scores = _dot(q_prev, k_ref[...])
        if project:
            q_ref[cur] = _dot(hn_ref[...], wq_all[head]).astype(BF16)
        scores = scores * (q_prev.shape[-1] ** -0.5)
        p = jnp.exp(scores - jnp.max(scores, axis=-1, keepdims=True))
        p = p / jnp.sum(p, axis=-1, keepdims=True)
        o = _dot(p.astype(BF16), v_ref[...])
        o = jnp.where(s > 0, o, 0.0)
        buf_ref[prev_slot] += _dot(o.astype(BF16), wo_all[prev_head])

    @pl.when(s < n_tiles * XA_HEADS)
    def _():
        step(project=True)

    @pl.when(s == n_tiles * XA_HEADS)
    def _():
        step(project=False)
        store(n_tiles - 1).start(priority=ROW_TILE_DMA_THREAD)
        store(n_tiles - 1).wait()


def _xattn(h, g, w_q, k, v, w_o, *, seq, n_mem, tm):
    m, d = h.shape
    hdim = d // XA_HEADS
    tiles_per_seq = seq // tm
    n_tiles = m // tm
    n_steps = n_tiles * XA_HEADS
    assert m % tm == 0 and seq % tm == 0 and XA_HEADS >= 3

    def prev_tile(s):
        return jnp.maximum(s - 1, 0) // XA_HEADS

    def prev_head(s):
        return jnp.maximum(s - 1, 0) % XA_HEADS

    vmem = (2 * tm * d * 4 + tm * d * 2 + 2 * tm * hdim * 2
            + 2 * 2 * d * hdim * w_q.dtype.itemsize + 2 * 2 * n_mem * hdim * 2
            + 2 * d * d * 2)
    return pl.pallas_call(
        functools.partial(_xattn_kernel, n_tiles=n_tiles),
        grid=(n_steps + 1,),
        in_specs=[
            pl.BlockSpec(memory_space=pl.ANY),
            pl.BlockSpec((1, d), lambda s: (0, 0)),
            pl.BlockSpec((d, hdim), lambda s: (0, jnp.minimum(s, XA_HEADS - 1))),
            pl.BlockSpec((hdim, n_mem), lambda s: (prev_head(s), prev_tile(s) // tiles_per_seq)),
            pl.BlockSpec((n_mem, hdim), lambda s: (prev_tile(s) // tiles_per_seq, prev_head(s))),
            pl.BlockSpec((hdim, d), lambda s: (prev_head(jnp.minimum(s, XA_HEADS)), 0)),
        ],
        out_specs=pl.BlockSpec(memory_space=pl.ANY),
        out_shape=jax.ShapeDtypeStruct((m, d), F32),
        scratch_shapes=[pltpu.VMEM((2, tm, d), F32), pltpu.VMEM((tm, d), BF16),
                        pltpu.VMEM((2, tm, hdim), BF16),
                        pltpu.VMEM((XA_HEADS, d, hdim), BF16), pltpu.VMEM((XA_HEADS, hdim, d), BF16),
                        pltpu.SemaphoreType.DMA(()), pltpu.SemaphoreType.DMA(())],
        compiler_params=pltpu.CompilerParams(
            dimension_semantics=("arbitrary",), vmem_limit_bytes=_vmem_limit(vmem)),
        name="xattn",
    )(h, g, w_q, k, v, w_o)


def kernel(x, mem, g_ffn1, w_ffn1_in, w_ffn1_out, g_mix, w_mix_in, conv_w, conv_b, g_gm_v,
           w_spatial, b_spatial, w_mix_out, g_xattn, g_mem, w_xq, w_xk, w_xv, w_xo, g_ffn2,
           w_ffn2_in, w_ffn2_out, g_final):
    b, s, d = x.shape
    n_mem = mem.shape[1]
    depth = g_ffn1.shape[0]
    assert depth >= 1
    heads_per_step = MIX_COLS // GROUP_DIM
    ffn_tiles = dict(tm=2048, tf=256, row_chunk=1024)
    tm = 1024

    def row(v):
        return v.reshape(1, -1)

    h = x.reshape(b * s, d)
    mem2 = mem.reshape(b * n_mem, d)
    g_fin = row(g_final)
    for l in range(depth):
        last = l == depth - 1
        h = _ffn(h, row(g_ffn1[l]), w_ffn1_in[l], w_ffn1_out[l], g_fin,
                 final_norm=False, **ffn_tiles)
        b_s_t = b_spatial[l].reshape(-1, heads_per_step, CHUNK).transpose(0, 2, 1)
        h = _mix(h, row(g_mix[l]), w_mix_in[l], conv_w[l], row(conv_b[l]), row(g_gm_v[l]),
                 w_spatial[l], b_s_t, w_mix_out[l], seq=s, tm=tm)
        k, v = _kv(mem2, row(g_mem[l]), w_xk[l], w_xv[l], tn=512)
        h = _xattn(h, row(g_xattn[l]), w_xq[l], k, v, w_xo[l], seq=s, n_mem=n_mem, tm=tm)
        h = _ffn(h, row(g_ffn2[l]), w_ffn2_in[l], w_ffn2_out[l], g_fin,
                 final_norm=last, **ffn_tiles)
    return h.reshape(b, s, d)
```

```python
import functools

import jax
import jax.numpy as jnp
from jax import lax
from jax.experimental import pallas as pl
from jax.experimental.pallas import tpu as pltpu

GROUP_DIM = 128
CHUNK = 128
CONV_K = 3
XA_HEADS = 4
EPS = 1e-6

V7X_VMEM_BYTES = 64 * 1024 * 1024
SUBLANES = 8
MIX_COLS = 256
ROW_TILE_DMA_THREAD = 1
NORM_ROWS = 256

BF16 = jnp.bfloat16
F32 = jnp.float32


def _rmsnorm(x, g):
    y = x * lax.rsqrt(jnp.mean(x * x, axis=-1, keepdims=True) + EPS)
    return y * g


_dot = functools.partial(jnp.dot, preferred_element_type=F32)


TEMPORARIES_BYTES = 8 << 20
VMEM_RESERVE_BYTES = 4 << 20


def _vmem_limit(buffer_bytes):
    return min(buffer_bytes + buffer_bytes // 4 + TEMPORARIES_BYTES,
               V7X_VMEM_BYTES - VMEM_RESERVE_BYTES)


def _ffn_kernel(x_hbm, g_ref, win_hbm, wout_hbm, gf_ref, o_hbm, buf_ref, xn_ref, wg_ref, wu_ref,
                wo_ref, in_sem, out_sem, w_sem, *, final_norm, row_chunk):
    i = pl.program_id(0)
    n_tiles = pl.num_programs(0)
    tm, d = xn_ref.shape
    tf = wo_ref.shape[1]
    d_ff = wout_hbm.shape[0]
    nk = d_ff // tf
    slot = lax.rem(i, 2)
    other = 1 - slot

    def load(tile, dst):
        return pltpu.make_async_copy(x_hbm.at[pl.ds(tile * tm, tm)], buf_ref.at[dst], in_sem)

    def store(tile, src):
        return pltpu.make_async_copy(buf_ref.at[src], o_hbm.at[pl.ds(tile * tm, tm)], out_sem)

    def weight_copies(k, ws):
        cols = pl.ds(pl.multiple_of(k * tf, tf), tf)
        up_cols = pl.ds(pl.multiple_of(d_ff + k * tf, tf), tf)
        return (pltpu.make_async_copy(win_hbm.at[:, cols], wg_ref.at[ws], w_sem.at[0, ws]),
                pltpu.make_async_copy(win_hbm.at[:, up_cols], wu_ref.at[ws], w_sem.at[1, ws]),
                pltpu.make_async_copy(wout_hbm.at[cols, :], wo_ref.at[ws], w_sem.at[2, ws]))

    @pl.when(i == 0)
    def _():
        load(0, 0).start(priority=ROW_TILE_DMA_THREAD)
        for c in weight_copies(0, 0):
            c.start()

    load(i, slot).wait()
    for rows in (pl.ds(r * NORM_ROWS, NORM_ROWS) for r in range(tm // NORM_ROWS)):
        xn_ref[rows, :] = _rmsnorm(buf_ref[slot, rows, :], g_ref[...]).astype(BF16)

    @pl.when(i > 0)
    def _():
        store(i - 1, other).start(priority=ROW_TILE_DMA_THREAD)

    def block(k, carry):
        @pl.when(k == nk // 2)
        def _():
            @pl.when(i > 0)
            def _():
                store(i - 1, other).wait()

            @pl.when(i + 1 < n_tiles)
            def _():
                load(i + 1, other).start(priority=ROW_TILE_DMA_THREAD)

        ws = lax.rem(i * nk + k, 2)
        for c in weight_copies(k, ws):
            c.wait()

        for c in weight_copies(lax.rem(k + 1, nk), 1 - ws):
            c.start()

        w_gate, w_up, w_down = (r[ws].astype(BF16) for r in (wg_ref, wu_ref, wo_ref))
        for rows in (pl.ds(r * row_chunk, row_chunk) for r in range(tm // row_chunk)):
            xn = xn_ref[rows, :]
            gate = _dot(xn, w_gate)
            up = _dot(xn, w_up)
            hidden = (0.5 * gate * jax.nn.sigmoid(gate) * up).astype(BF16)
            buf_ref[slot, rows, :] += _dot(hidden, w_down)
        return carry

    lax.fori_loop(0, nk, block, 0)

    if final_norm:
        for rows in (pl.ds(r * NORM_ROWS, NORM_ROWS) for r in range(tm // NORM_ROWS)):
            buf_ref[slot, rows, :] = _rmsnorm(buf_ref[slot, rows, :], gf_ref[...])

    @pl.when(i == n_tiles - 1)
    def _():
        store(i, slot).start(priority=ROW_TILE_DMA_THREAD)
        for c in weight_copies(0, lax.rem(n_tiles * nk, 2)):
            c.wait()
        store(i, slot).wait()


def _ffn(x, g, w_in, w_out, g_final, *, final_norm, tm, tf, row_chunk):
    m, d = x.shape
    d_ff = w_out.shape[0]
    nk = d_ff // tf
    assert m % tm == 0 and d_ff % tf == 0 and nk >= 2 and tm % row_chunk == 0
    vmem = (2 * tm * d * 4
            + tm * d * 2
            + 2 * 3 * d * tf * w_in.dtype.itemsize)
    return pl.pallas_call(
        functools.partial(_ffn_kernel, final_norm=final_norm, row_chunk=row_chunk),
        grid=(m // tm,),
        in_specs=[
            pl.BlockSpec(memory_space=pl.ANY),
            pl.BlockSpec((1, d), lambda i: (0, 0)),
            pl.BlockSpec(memory_space=pl.ANY),
            pl.BlockSpec(memory_space=pl.ANY),
            pl.BlockSpec((1, d), lambda i: (0, 0)),
        ],
        out_specs=pl.BlockSpec(memory_space=pl.ANY),
        out_shape=jax.ShapeDtypeStruct((m, d), F32),
        scratch_shapes=[pltpu.VMEM((2, tm, d), F32), pltpu.VMEM((tm, d), BF16),
                        pltpu.VMEM((2, d, tf), w_in.dtype), pltpu.VMEM((2, d, tf), w_in.dtype),
                        pltpu.VMEM((2, tf, d), w_out.dtype),
                        pltpu.SemaphoreType.DMA(()), pltpu.SemaphoreType.DMA(()),
                        pltpu.SemaphoreType.DMA((3, 2))],
        compiler_params=pltpu.CompilerParams(
            dimension_semantics=("arbitrary",), vmem_limit_bytes=_vmem_limit(vmem)),
        name="ffn_final" if final_norm else "ffn",
    )(x, g, w_in, w_out, g_final)


def _stream_row_tiles(s, per, n_tiles, h_hbm, o_hbm, buf_ref, in_sem, out_sem):
    tm = buf_ref.shape[1]
    t = s // per
    j = lax.rem(s, per)

    def load(tile):
        return pltpu.make_async_copy(h_hbm.at[pl.ds(tile * tm, tm)],
                                     buf_ref.at[lax.rem(tile, 2)], in_sem)

    def store(tile):
        return pltpu.make_async_copy(buf_ref.at[lax.rem(tile, 2)],
                                     o_hbm.at[pl.ds(tile * tm, tm)], out_sem)

    @pl.when(s == 0)
    def _():
        load(0).start(priority=ROW_TILE_DMA_THREAD)

    @pl.when((j == 0) & (t < n_tiles))
    def _():
        load(t).wait()

    @pl.when((j == 1) & (t >= 1))
    def _():
        store(t - 1).start(priority=ROW_TILE_DMA_THREAD)

    @pl.when(j == 2)
    def _():
        @pl.when(t >= 1)
        def _():
            store(t - 1).wait()

        @pl.when(t + 1 < n_tiles)
        def _():
            load(t + 1).start(priority=ROW_TILE_DMA_THREAD)

    return store


def _mix_kernel(h_hbm, g_ref, wb_ref, wc_ref, wh_ref, wu_ref, wv_ref, cw_ref, cb_ref,
                gv_ref, ws_ref, bs_ref, woc_ref, wog_ref, o_hbm, buf_ref, hn_ref, y_ref, carry_ref,
                in_sem, out_sem, *, nj, n_tiles, tiles_per_seq):
    s = pl.program_id(0)
    j = lax.rem(s, nj)
    cur = lax.rem(s, 2)
    prev = 1 - cur
    tm = hn_ref.shape[0]
    tile_slot = lax.rem(s // nj, 2)
    prev_slot = lax.rem(jnp.maximum(s - 1, 0) // nj, 2)
    store = _stream_row_tiles(s, nj, n_tiles, h_hbm, o_hbm, buf_ref, in_sem, out_sem)

    @pl.when(s == 0)
    def _():
        y_ref[prev] = jnp.zeros(y_ref.shape[1:], BF16)

    @pl.when((j == 0) & (s < n_tiles * nj))
    def _():
        hn_ref[...] = _rmsnorm(buf_ref[tile_slot], g_ref[...]).astype(BF16)

    @pl.when(lax.rem(s // nj, tiles_per_seq) == 0)
    def _():
        carry_ref[j] = jnp.zeros((SUBLANES, MIX_COLS), F32)

    def step(compute_block):
        if compute_block:
            w_in = jnp.concatenate(
                [r[...].astype(BF16) for r in (wv_ref, wu_ref, wc_ref, wh_ref, wb_ref)], axis=1)
            proj = _dot(hn_ref[...], w_in)
            v, u, gate_c, h_c, gate_b = (proj[:, n * MIX_COLS:(n + 1) * MIX_COLS]
                                         for n in range(5))

        y_prev = y_ref[prev]
        buf_ref[prev_slot] += _dot(y_prev[:, :MIX_COLS], woc_ref[...].astype(BF16))

        if compute_block:
            gv = gv_ref[...]
            bs = bs_ref[...]
            tri = (lax.broadcasted_iota(jnp.int32, (CHUNK, CHUNK), 0)
                   >= lax.broadcasted_iota(jnp.int32, (CHUNK, CHUNK), 1))
            sgs = []
            for hh in range(MIX_COLS // GROUP_DIM):
                lanes = slice(hh * GROUP_DIM, (hh + 1) * GROUP_DIM)
                vh = _rmsnorm(v[:, lanes], gv[:, lanes]).astype(BF16)
                w = jnp.where(tri, ws_ref[hh], 0.0).astype(BF16)
                sgs.append([_dot(w, vh[c * CHUNK:(c + 1) * CHUNK]) for c in range(tm // CHUNK)])

        buf_ref[prev_slot] += _dot(y_prev[:, MIX_COLS:], wog_ref[...].astype(BF16))

        if not compute_block:
            return
        for hh, per_chunk in enumerate(sgs):
            lanes = slice(hh * GROUP_DIM, (hh + 1) * GROUP_DIM)
            for c, sg in enumerate(per_chunk):
                rows = slice(c * CHUNK, (c + 1) * CHUNK)
                y_ref[cur, rows, MIX_COLS + hh * GROUP_DIM:MIX_COLS + (hh + 1) * GROUP_DIM] = (
                    u[rows, lanes] * (sg + bs[:, hh:hh + 1])).astype(BF16)

        z = gate_c * h_c
        tail = carry_ref[j]
        carry_ref[j] = z[tm - SUBLANES:, :]
        cw = cw_ref[...]
        cb = cb_ref[...]
        row = lax.broadcasted_iota(jnp.int32, (SUBLANES, MIX_COLS), 0)
        z1 = pltpu.roll(z, 1, 0)
        z2 = pltpu.roll(z, 2, 0)
        head1 = jnp.where(row < 1, pltpu.roll(tail, 1, 0), z1[:SUBLANES])
        head2 = jnp.where(row < 2, pltpu.roll(tail, 2, 0), z2[:SUBLANES])
        z1 = jnp.concatenate([head1, z1[SUBLANES:]], axis=0)
        z2 = jnp.concatenate([head2, z2[SUBLANES:]], axis=0)
        y_ref[cur, :, :MIX_COLS] = (
            gate_b * (cb + cw[2:3] * z + cw[1:2] * z1 + cw[0:1] * z2)).astype(BF16)

    @pl.when(s < n_tiles * nj)
    def _():
        step(compute_block=True)

    @pl.when(s == n_tiles * nj)
    def _():
        step(compute_block=False)
        store(n_tiles - 1).start(priority=ROW_TILE_DMA_THREAD)
        store(n_tiles - 1).wait()


def _mix(h, g, w_in, conv_w, conv_b, g_v, w_s, b_s_t, w_out, *, seq, tm):
    m, d = h.shape
    width = w_out.shape[0]
    conv_width = width // 2
    nj = conv_width // MIX_COLS
    heads_per_step = MIX_COLS // GROUP_DIM
    n_tiles = m // tm
    n_steps = n_tiles * nj
    assert m % tm == 0 and seq % tm == 0 and tm % CHUNK == 0 and nj >= 3

    def prev_j(s):
        return jnp.maximum(s - 1, 0) % nj

    def col_block(offset):
        return pl.BlockSpec((d, MIX_COLS), lambda s: (0, offset + s % nj))

    vmem = (2 * tm * d * 4 + tm * d * 2 + 2 * tm * 2 * MIX_COLS * 2
            + 2 * 5 * d * MIX_COLS * w_in.dtype.itemsize
            + 2 * 2 * MIX_COLS * d * w_out.dtype.itemsize)
    return pl.pallas_call(
        functools.partial(_mix_kernel, nj=nj, n_tiles=n_tiles, tiles_per_seq=seq // tm),
        grid=(n_steps + 1,),
        in_specs=[
            pl.BlockSpec(memory_space=pl.ANY),
            pl.BlockSpec((1, d), lambda s: (0, 0)),
            col_block(0), col_block(nj), col_block(2 * nj),
            col_block(3 * nj), col_block(4 * nj),
            pl.BlockSpec((CONV_K, MIX_COLS), lambda s: (0, s % nj)),
            pl.BlockSpec((1, MIX_COLS), lambda s: (0, s % nj)),
            pl.BlockSpec((1, MIX_COLS), lambda s: (0, s % nj)),
            pl.BlockSpec((heads_per_step, CHUNK, CHUNK), lambda s: (s % nj, 0, 0)),
            pl.BlockSpec((None, CHUNK, heads_per_step), lambda s: (s % nj, 0, 0)),
            pl.BlockSpec((MIX_COLS, d), lambda s: (prev_j(s), 0)),
            pl.BlockSpec((MIX_COLS, d), lambda s: (nj + prev_j(s), 0)),
        ],
        out_specs=pl.BlockSpec(memory_space=pl.ANY),
        out_shape=jax.ShapeDtypeStruct((m, d), F32),
        scratch_shapes=[pltpu.VMEM((2, tm, d), F32), pltpu.VMEM((tm, d), BF16),
                        pltpu.VMEM((2, tm, 2 * MIX_COLS), BF16),
                        pltpu.VMEM((nj, SUBLANES, MIX_COLS), F32),
                        pltpu.SemaphoreType.DMA(()), pltpu.SemaphoreType.DMA(())],
        compiler_params=pltpu.CompilerParams(
            dimension_semantics=("arbitrary",), vmem_limit_bytes=_vmem_limit(vmem)),
        name="mix",
    )(h, g, w_in, w_in, w_in, w_in, w_in, conv_w, conv_b, g_v, w_s, b_s_t, w_out, w_out)


def _kv_kernel(mem_ref, g_ref, wk_ref, wv_ref, k_ref, v_ref, mn_ref):
    @pl.when(pl.program_id(0) == 0)
    def _():
        mn_ref[...] = _rmsnorm(mem_ref[...], g_ref[...]).astype(BF16)

    mn = mn_ref[...]
    k_ref[...] = _dot(mn, wk_ref[...].astype(BF16)).T.astype(BF16)
    v_ref[...] = _dot(mn, wv_ref[...].astype(BF16)).astype(BF16)


def _kv(mem, g, w_k, w_v, *, tn):
    m, d = mem.shape
    vmem = (2 * m * d * 4 + m * d * 2 + 2 * 2 * d * tn * w_k.dtype.itemsize
            + 2 * 2 * m * tn * 2)
    return pl.pallas_call(
        _kv_kernel,
        grid=(d // tn,),
        in_specs=[
            pl.BlockSpec((m, d), lambda n: (0, 0)),
            pl.BlockSpec((1, d), lambda n: (0, 0)),
            pl.BlockSpec((d, tn), lambda n: (0, n)),
            pl.BlockSpec((d, tn), lambda n: (0, n)),
        ],
        out_specs=[pl.BlockSpec((tn, m), lambda n: (n, 0)), pl.BlockSpec((m, tn), lambda n: (0, n))],
        out_shape=[jax.ShapeDtypeStruct((d, m), BF16), jax.ShapeDtypeStruct((m, d), BF16)],
        scratch_shapes=[pltpu.VMEM((m, d), BF16)],
        compiler_params=pltpu.CompilerParams(
            dimension_semantics=("arbitrary",), vmem_limit_bytes=_vmem_limit(vmem)),
        name="kv",
    )(mem, g, w_k, w_v)


def _xattn_kernel(h_hbm, g_ref, wq_ref, k_ref, v_ref, wo_ref, o_hbm, buf_ref, hn_ref, q_ref,
                  wq_all, wo_all, in_sem, out_sem, *, n_tiles):
    s = pl.program_id(0)
    cur = lax.rem(s, 2)
    prev = 1 - cur
    head = lax.rem(s, XA_HEADS)
    prev_head = lax.rem(jnp.maximum(s - 1, 0), XA_HEADS)
    tile_slot = lax.rem(s // XA_HEADS, 2)
    prev_slot = lax.rem(jnp.maximum(s - 1, 0) // XA_HEADS, 2)
    store = _stream_row_tiles(s, XA_HEADS, n_tiles, h_hbm, o_hbm, buf_ref, in_sem, out_sem)

    @pl.when(s == 0)
    def _():
        q_ref[prev] = jnp.zeros(q_ref.shape[1:], BF16)

    @pl.when(s < XA_HEADS)
    def _():
        wq_all[head] = wq_ref[...].astype(BF16)

    @pl.when(s <= XA_HEADS)
    def _():
        wo_all[prev_head] = wo_ref[...].astype(BF16)

    @pl.when((lax.rem(s, XA_HEADS) == 0) & (s < n_tiles * XA_HEADS))
    def _():
        hn_ref[...] = _rmsnorm(buf_ref[tile_slot], g_ref[...]).astype(BF16)

    def step(project):
        q_prev = q_ref[prev]
        scores = _dot(q_prev, k_ref[...])
        if project:
            q_ref[cur] = _dot(hn_ref[...], wq_all[head]).astype(BF16)
        scores = scores * (q_prev.shape[-1] ** -0.5)
        p = jnp.exp(scores - jnp.max(scores, axis=-1, keepdims=True))
        p = p / jnp.sum(p, axis=-1, keepdims=True)
        o = _dot(p.astype(BF16), v_ref[...])
        o = jnp.where(s > 0, o, 0.0)
        buf_ref[prev_slot] += _dot(o.astype(BF16), wo_all[prev_head])

    @pl.when(s < n_tiles * XA_HEADS)
    def _():
        step(project=True)

    @pl.when(s == n_tiles * XA_HEADS)
    def _():
        step(project=False)
        store(n_tiles - 1).start(priority=ROW_TILE_DMA_THREAD)
        store(n_tiles - 1).wait()


def _xattn(h, g, w_q, k, v, w_o, *, seq, n_mem, tm):
    m, d = h.shape
    hdim = d // XA_HEADS
    tiles_per_seq = seq // tm
    n_tiles = m // tm
    n_steps = n_tiles * XA_HEADS
    assert m % tm == 0 and seq % tm == 0 and XA_HEADS >= 3

    def prev_tile(s):
        return jnp.maximum(s - 1, 0) // XA_HEADS

    def prev_head(s):
        return jnp.maximum(s - 1, 0) % XA_HEADS

    vmem = (2 * tm * d * 4 + tm * d * 2 + 2 * tm * hdim * 2
            + 2 * 2 * d * hdim * w_q.dtype.itemsize + 2 * 2 * n_mem * hdim * 2
            + 2 * d * d * 2)
    return pl.pallas_call(
        functools.partial(_xattn_kernel, n_tiles=n_tiles),
        grid=(n_steps + 1,),
        in_specs=[
            pl.BlockSpec(memory_space=pl.ANY),
            pl.BlockSpec((1, d), lambda s: (0, 0)),
            pl.BlockSpec((d, hdim), lambda s: (0, jnp.minimum(s, XA_HEADS - 1))),
            pl.BlockSpec((hdim, n_mem), lambda s: (prev_head(s), prev_tile(s) // tiles_per_seq)),
            pl.BlockSpec((n_mem, hdim), lambda s: (prev_tile(s) // tiles_per_seq, prev_head(s))),
            pl.BlockSpec((hdim, d), lambda s: (prev_head(jnp.minimum(s, XA_HEADS)), 0)),
        ],
        out_specs=pl.BlockSpec(memory_space=pl.ANY),
        out_shape=jax.ShapeDtypeStruct((m, d), F32),
        scratch_shapes=[pltpu.VMEM((2, tm, d), F32), pltpu.VMEM((tm, d), BF16),
                        pltpu.VMEM((2, tm, hdim), BF16),
                        pltpu.VMEM((XA_HEADS, d, hdim), BF16), pltpu.VMEM((XA_HEADS, hdim, d), BF16),
                        pltpu.SemaphoreType.DMA(()), pltpu.SemaphoreType.DMA(())],
        compiler_params=pltpu.CompilerParams(
            dimension_semantics=("arbitrary",), vmem_limit_bytes=_vmem_limit(vmem)),
        name="xattn",
    )(h, g, w_q, k, v, w_o)


def kernel(x, mem, g_ffn1, w_ffn1_in, w_ffn1_out, g_mix, w_mix_in, conv_w, conv_b, g_gm_v,
           w_spatial, b_spatial, w_mix_out, g_xattn, g_mem, w_xq, w_xk, w_xv, w_xo, g_ffn2,
           w_ffn2_in, w_ffn2_out, g_final):
    b, s, d = x.shape
    n_mem = mem.shape[1]
    depth = g_ffn1.shape[0]
    assert depth >= 1
    heads_per_step = MIX_COLS // GROUP_DIM
    ffn_tiles = dict(tm=2048, tf=256, row_chunk=1024)
    tm = 1024

    def row(v):
        return v.reshape(1, -1)

    h = x.reshape(b * s, d)
    mem2 = mem.reshape(b * n_mem, d)
    g_fin = row(g_final)
    for l in range(depth):
        last = l == depth - 1
        h = _ffn(h, row(g_ffn1[l]), w_ffn1_in[l], w_ffn1_out[l], g_fin,
                 final_norm=False, **ffn_tiles)
        b_s_t = b_spatial[l].reshape(-1, heads_per_step, CHUNK).transpose(0, 2, 1)
        h = _mix(h, row(g_mix[l]), w_mix_in[l], conv_w[l], row(conv_b[l]), row(g_gm_v[l]),
                 w_spatial[l], b_s_t, w_mix_out[l], seq=s, tm=tm)
        k, v = _kv(mem2, row(g_mem[l]), w_xk[l], w_xv[l], tn=512)
        h = _xattn(h, row(g_xattn[l]), w_xq[l], k, v, w_xo[l], seq=s, n_mem=n_mem, tm=tm)
        h = _ffn(h, row(g_ffn2[l]), w_ffn2_in[l], w_ffn2_out[l], g_fin,
                 final_norm=last, **ffn_tiles)
    return h.reshape(b, s, d)
```

```python
import functools

import jax
import jax.numpy as jnp
from jax import lax
from jax.experimental import pallas as pl
from jax.experimental.pallas import tpu as pltpu

GROUP_DIM = 128
CHUNK = 128
CONV_K = 3
XA_HEADS = 4
EPS = 1e-6

V7X_VMEM_BYTES = 64 * 1024 * 1024
SUBLANES = 8
MIX_COLS = 256
ROW_TILE_DMA_THREAD = 1
NORM_ROWS = 256

BF16 = jnp.bfloat16
F32 = jnp.float32


def _rmsnorm(x, g):
    y = x * lax.rsqrt(jnp.mean(x * x, axis=-1, keepdims=True) + EPS)
    return y * g


_dot = functools.partial(jnp.dot, preferred_element_type=F32)


TEMPORARIES_BYTES = 8 << 20
VMEM_RESERVE_BYTES = 4 << 20


def _vmem_limit(buffer_bytes):
    return min(buffer_bytes + buffer_bytes // 4 + TEMPORARIES_BYTES,
               V7X_VMEM_BYTES - VMEM_RESERVE_BYTES)


def _ffn_kernel(x_hbm, g_ref, win_hbm, wout_hbm, gf_ref, o_hbm, buf_ref, xn_ref, wg_ref, wu_ref,
                wo_ref, in_sem, out_sem, w_sem, *, final_norm, row_chunk):
    i = pl.program_id(0)
    n_tiles = pl.num_programs(0)
    tm, d = xn_ref.shape
    tf = wo_ref.shape[1]
    d_ff = wout_hbm.shape[0]
    nk = d_ff // tf
    slot = lax.rem(i, 2)
    other = 1 - slot

    def load(tile, dst):
        return pltpu.make_async_copy(x_hbm.at[pl.ds(tile * tm, tm)], buf_ref.at[dst], in_sem)

    def store(tile, src):
        return pltpu.make_async_copy(buf_ref.at[src], o_hbm.at[pl.ds(tile * tm, tm)], out_sem)

    def weight_copies(k, ws):
        cols = pl.ds(pl.multiple_of(k * tf, tf), tf)
        up_cols = pl.ds(pl.multiple_of(d_ff + k * tf, tf), tf)
        return (pltpu.make_async_copy(win_hbm.at[:, cols], wg_ref.at[ws], w_sem.at[0, ws]),
                pltpu.make_async_copy(win_hbm.at[:, up_cols], wu_ref.at[ws], w_sem.at[1, ws]),
                pltpu.make_async_copy(wout_hbm.at[cols, :], wo_ref.at[ws], w_sem.at[2, ws]))

    @pl.when(i == 0)
    def _():
        load(0, 0).start(priority=ROW_TILE_DMA_THREAD)
        for c in weight_copies(0, 0):
            c.start()

    load(i, slot).wait()
    for rows in (pl.ds(r * NORM_ROWS, NORM_ROWS) for r in range(tm // NORM_ROWS)):
        xn_ref[rows, :] = _rmsnorm(buf_ref[slot, rows, :], g_ref[...]).astype(BF16)

    @pl.when(i > 0)
    def _():
        store(i - 1, other).start(priority=ROW_TILE_DMA_THREAD)

    def block(k, carry):
        @pl.when(k == nk // 2)
        def _():
            @pl.when(i > 0)
            def _():
                store(i - 1, other).wait()

            @pl.when(i + 1 < n_tiles)
            def _():
                load(i + 1, other).start(priority=ROW_TILE_DMA_THREAD)

        ws = lax.rem(i * nk + k, 2)
        for c in weight_copies(k, ws):
            c.wait()

        for c in weight_copies(lax.rem(k + 1, nk), 1 - ws):
            c.start()

        w_gate, w_up, w_down = (r[ws].astype(BF16) for r in (wg_ref, wu_ref, wo_ref))
        for rows in (pl.ds(r * row_chunk, row_chunk) for r in range(tm // row_chunk)):
            xn = xn_ref[rows, :]
            gate = _dot(xn, w_gate)
            up = _dot(xn, w_up)
            hidden = (0.5 * gate * jax.nn.sigmoid(gate) * up).astype(BF16)
            buf_ref[slot, rows, :] += _dot(hidden, w_down)
        return carry

    lax.fori_loop(0, nk, block, 0)

    if final_norm:
        for rows in (pl.ds(r * NORM_ROWS, NORM_ROWS) for r in range(tm // NORM_ROWS)):
            buf_ref[slot, rows, :] = _rmsnorm(buf_ref[slot, rows, :], gf_ref[...])

    @pl.when(i == n_tiles - 1)
    def _():
        store(i, slot).start(priority=ROW_TILE_DMA_THREAD)
        for c in weight_copies(0, lax.rem(n_tiles * nk, 2)):
            c.wait()
        store(i, slot).wait()


def _ffn(x, g, w_in, w_out, g_final, *, final_norm, tm, tf, row_chunk):
    m, d = x.shape
    d_ff = w_out.shape[0]
    nk = d_ff // tf
    assert m % tm == 0 and d_ff % tf == 0 and nk >= 2 and tm % row_chunk == 0
    vmem = (2 * tm * d * 4
            + tm * d * 2
            + 2 * 3 * d * tf * w_in.dtype.itemsize)
    return pl.pallas_call(
        functools.partial(_ffn_kernel, final_norm=final_norm, row_chunk=row_chunk),
        grid=(m // tm,),
        in_specs=[
            pl.BlockSpec(memory_space=pl.ANY),
            pl.BlockSpec((1, d), lambda i: (0, 0)),
            pl.BlockSpec(memory_space=pl.ANY),
            pl.BlockSpec(memory_space=pl.ANY),
            pl.BlockSpec((1, d), lambda i: (0, 0)),
        ],
        out_specs=pl.BlockSpec(memory_space=pl.ANY),
        out_shape=jax.ShapeDtypeStruct((m, d), F32),
        scratch_shapes=[pltpu.VMEM((2, tm, d), F32), pltpu.VMEM((tm, d), BF16),
                        pltpu.VMEM((2, d, tf), w_in.dtype), pltpu.VMEM((2, d, tf), w_in.dtype),
                        pltpu.VMEM((2, tf, d), w_out.dtype),
                        pltpu.SemaphoreType.DMA(()), pltpu.SemaphoreType.DMA(()),
                        pltpu.SemaphoreType.DMA((3, 2))],
        compiler_params=pltpu.CompilerParams(
            dimension_semantics=("arbitrary",), vmem_limit_bytes=_vmem_limit(vmem)),
        name="ffn_final" if final_norm else "ffn",
    )(x, g, w_in, w_out, g_final)


def _stream_row_tiles(s, per, n_tiles, h_hbm, o_hbm, buf_ref, in_sem, out_sem):
    tm = buf_ref.shape[1]
    t = s // per
    j = lax.rem(s, per)

    def load(tile):
        return pltpu.make_async_copy(h_hbm.at[pl.ds(tile * tm, tm)],
                                     buf_ref.at[lax.rem(tile, 2)], in_sem)

    def store(tile):
        return pltpu.make_async_copy(buf_ref.at[lax.rem(tile, 2)],
                                     o_hbm.at[pl.ds(tile * tm, tm)], out_sem)

    @pl.when(s == 0)
    def _():
        load(0).start(priority=ROW_TILE_DMA_THREAD)

    @pl.when((j == 0) & (t < n_tiles))
    def _():
        load(t).wait()

    @pl.when((j == 1) & (t >= 1))
    def _():
        store(t - 1).start(priority=ROW_TILE_DMA_THREAD)

    @pl.when(j == 2)
    def _():
        @pl.when(t >= 1)
        def _():
            store(t - 1).wait()

        @pl.when(t + 1 < n_tiles)
        def _():
            load(t + 1).start(priority=ROW_TILE_DMA_THREAD)

    return store


def _mix_kernel(h_hbm, g_ref, wb_ref, wc_ref, wh_ref, wu_ref, wv_ref, cw_ref, cb_ref,
                gv_ref, ws_ref, bs_ref, woc_ref, wog_ref, o_hbm, buf_ref, hn_ref, y_ref, carry_ref,
                in_sem, out_sem, *, nj, n_tiles, tiles_per_seq):
    s = pl.program_id(0)
    j = lax.rem(s, nj)
    cur = lax.rem(s, 2)
    prev = 1 - cur
    tm = hn_ref.shape[0]
    tile_slot = lax.rem(s // nj, 2)
    prev_slot = lax.rem(jnp.maximum(s - 1, 0) // nj, 2)
    store = _stream_row_tiles(s, nj, n_tiles, h_hbm, o_hbm, buf_ref, in_sem, out_sem)

    @pl.when(s == 0)
    def _():
        y_ref[prev] = jnp.zeros(y_ref.shape[1:], BF16)

    @pl.when((j == 0) & (s < n_tiles * nj))
    def _():
        for rows in (pl.ds(r * NORM_ROWS, NORM_ROWS) for r in range(hn_ref.shape[0] // NORM_ROWS)):
            hn_ref[rows, :] = _rmsnorm(buf_ref[tile_slot, rows, :], g_ref[...]).astype(BF16)

    @pl.when(lax.rem(s // nj, tiles_per_seq) == 0)
    def _():
        carry_ref[j] = jnp.zeros((SUBLANES, MIX_COLS), F32)

    def step(compute_block):
        if compute_block:
            w_in = jnp.concatenate(
                [r[...].astype(BF16) for r in (wv_ref, wu_ref, wc_ref, wh_ref, wb_ref)], axis=1)
            proj = _dot(hn_ref[...], w_in)
            v, u, gate_c, h_c, gate_b = (proj[:, n * MIX_COLS:(n + 1) * MIX_COLS]
                                         for n in range(5))

        y_prev = y_ref[prev]
        buf_ref[prev_slot] += _dot(y_prev[:, :MIX_COLS], woc_ref[...].astype(BF16))

        if compute_block:
            gv = gv_ref[...]
            bs = bs_ref[...]
            tri = (lax.broadcasted_iota(jnp.int32, (CHUNK, CHUNK), 0)
                   >= lax.broadcasted_iota(jnp.int32, (CHUNK, CHUNK), 1))
            sgs = []
            for hh in range(MIX_COLS // GROUP_DIM):
                lanes = slice(hh * GROUP_DIM, (hh + 1) * GROUP_DIM)
                vh = _rmsnorm(v[:, lanes], gv[:, lanes]).astype(BF16)
                w = jnp.where(tri, ws_ref[hh], 0.0).astype(BF16)
                side_by_side = jnp.concatenate(
                    [vh[c * CHUNK:(c + 1) * CHUNK] for c in range(tm // CHUNK)], axis=1)
                sg = _dot(w, side_by_side)
                sgs.append([sg[:, c * GROUP_DIM:(c + 1) * GROUP_DIM] for c in range(tm // CHUNK)])

        buf_ref[prev_slot] += _dot(y_prev[:, MIX_COLS:], wog_ref[...].astype(BF16))

        if not compute_block:
            return
        for hh, per_chunk in enumerate(sgs):
            lanes = slice(hh * GROUP_DIM, (hh + 1) * GROUP_DIM)
            for c, sg in enumerate(per_chunk):
                rows = slice(c * CHUNK, (c + 1) * CHUNK)
                y_ref[cur, rows, MIX_COLS + hh * GROUP_DIM:MIX_COLS + (hh + 1) * GROUP_DIM] = (
                    u[rows, lanes] * (sg + bs[:, hh:hh + 1])).astype(BF16)

        z = gate_c * h_c
        tail = carry_ref[j]
        carry_ref[j] = z[tm - SUBLANES:, :]
        cw = cw_ref[...]
        cb = cb_ref[...]
        row = lax.broadcasted_iota(jnp.int32, (SUBLANES, MIX_COLS), 0)
        z1 = pltpu.roll(z, 1, 0)
        z2 = pltpu.roll(z, 2, 0)
        head1 = jnp.where(row < 1, pltpu.roll(tail, 1, 0), z1[:SUBLANES])
        head2 = jnp.where(row < 2, pltpu.roll(tail, 2, 0), z2[:SUBLANES])
        z1 = jnp.concatenate([head1, z1[SUBLANES:]], axis=0)
        z2 = jnp.concatenate([head2, z2[SUBLANES:]], axis=0)
        y_ref[cur, :, :MIX_COLS] = (
            gate_b * (cb + cw[2:3] * z + cw[1:2] * z1 + cw[0:1] * z2)).astype(BF16)

    @pl.when(s < n_tiles * nj)
    def _():
        step(compute_block=True)

    @pl.when(s == n_tiles * nj)
    def _():
        step(compute_block=False)
        store(n_tiles - 1).start(priority=ROW_TILE_DMA_THREAD)
        store(n_tiles - 1).wait()


def _mix(h, g, w_in, conv_w, conv_b, g_v, w_s, b_s_t, w_out, *, seq, tm):
    m, d = h.shape
    width = w_out.shape[0]
    conv_width = width // 2
    nj = conv_width // MIX_COLS
    heads_per_step = MIX_COLS // GROUP_DIM
    n_tiles = m // tm
    n_steps = n_tiles * nj
    assert m % tm == 0 and seq % tm == 0 and tm % CHUNK == 0 and nj >= 3

    def prev_j(s):
        return jnp.maximum(s - 1, 0) % nj

    def col_block(offset):
        return pl.BlockSpec((d, MIX_COLS), lambda s: (0, offset + s % nj))

    vmem = (2 * tm * d * 4 + tm * d * 2 + 2 * tm * 2 * MIX_COLS * 2
            + 2 * 5 * d * MIX_COLS * w_in.dtype.itemsize
            + 2 * 2 * MIX_COLS * d * w_out.dtype.itemsize)
    return pl.pallas_call(
        functools.partial(_mix_kernel, nj=nj, n_tiles=n_tiles, tiles_per_seq=seq // tm),
        grid=(n_steps + 1,),
        in_specs=[
            pl.BlockSpec(memory_space=pl.ANY),
            pl.BlockSpec((1, d), lambda s: (0, 0)),
            col_block(0), col_block(nj), col_block(2 * nj),
            col_block(3 * nj), col_block(4 * nj),
            pl.BlockSpec((CONV_K, MIX_COLS), lambda s: (0, s % nj)),
            pl.BlockSpec((1, MIX_COLS), lambda s: (0, s % nj)),
            pl.BlockSpec((1, MIX_COLS), lambda s: (0, s % nj)),
            pl.BlockSpec((heads_per_step, CHUNK, CHUNK), lambda s: (s % nj, 0, 0)),
            pl.BlockSpec((None, CHUNK, heads_per_step), lambda s: (s % nj, 0, 0)),
            pl.BlockSpec((MIX_COLS, d), lambda s: (prev_j(s), 0)),
            pl.BlockSpec((MIX_COLS, d), lambda s: (nj + prev_j(s), 0)),
        ],
        out_specs=pl.BlockSpec(memory_space=pl.ANY),
        out_shape=jax.ShapeDtypeStruct((m, d), F32),
        scratch_shapes=[pltpu.VMEM((2, tm, d), F32), pltpu.VMEM((tm, d), BF16),
                        pltpu.VMEM((2, tm, 2 * MIX_COLS), BF16),
                        pltpu.VMEM((nj, SUBLANES, MIX_COLS), F32),
                        pltpu.SemaphoreType.DMA(()), pltpu.SemaphoreType.DMA(())],
        compiler_params=pltpu.CompilerParams(
            dimension_semantics=("arbitrary",), vmem_limit_bytes=_vmem_limit(vmem)),
        name="mix",
    )(h, g, w_in, w_in, w_in, w_in, w_in, conv_w, conv_b, g_v, w_s, b_s_t, w_out, w_out)


def _kv_kernel(mem_ref, g_ref, wk_ref, wv_ref, k_ref, v_ref, mn_ref):
    @pl.when(pl.program_id(0) == 0)
    def _():
        mn_ref[...] = _rmsnorm(mem_ref[...], g_ref[...]).astype(BF16)

    mn = mn_ref[...]
    k_ref[...] = _dot(mn, wk_ref[...].astype(BF16)).T.astype(BF16)
    v_ref[...] = _dot(mn, wv_ref[...].astype(BF16)).astype(BF16)


def _kv(mem, g, w_k, w_v, *, tn):
    m, d = mem.shape
    vmem = (2 * m * d * 4 + m * d * 2 + 2 * 2 * d * tn * w_k.dtype.itemsize
            + 2 * 2 * m * tn * 2)
    return pl.pallas_call(
        _kv_kernel,
        grid=(d // tn,),
        in_specs=[
            pl.BlockSpec((m, d), lambda n: (0, 0)),
            pl.BlockSpec((1, d), lambda n: (0, 0)),
            pl.BlockSpec((d, tn), lambda n: (0, n)),
            pl.BlockSpec((d, tn), lambda n: (0, n)),
        ],
        out_specs=[pl.BlockSpec((tn, m), lambda n: (n, 0)), pl.BlockSpec((m, tn), lambda n: (0, n))],
        out_shape=[jax.ShapeDtypeStruct((d, m), BF16), jax.ShapeDtypeStruct((m, d), BF16)],
        scratch_shapes=[pltpu.VMEM((m, d), BF16)],
        compiler_params=pltpu.CompilerParams(
            dimension_semantics=("arbitrary",), vmem_limit_bytes=_vmem_limit(vmem)),
        name="kv",
    )(mem, g, w_k, w_v)


def _xattn_kernel(h_hbm, g_ref, wq_ref, k_ref, v_ref, wo_ref, o_hbm, buf_ref, hn_ref, q_ref,
                  wq_all, wo_all, in_sem, out_sem, *, n_tiles):
    s = pl.program_id(0)
    cur = lax.rem(s, 2)
    prev = 1 - cur
    head = lax.rem(s, XA_HEADS)
    prev_head = lax.rem(jnp.maximum(s - 1, 0), XA_HEADS)
    tile_slot = lax.rem(s // XA_HEADS, 2)
    prev_slot = lax.rem(jnp.maximum(s - 1, 0) // XA_HEADS, 2)
    store = _stream_row_tiles(s, XA_HEADS, n_tiles, h_hbm, o_hbm, buf_ref, in_sem, out_sem)

    @pl.when(s == 0)
    def _():
        q_ref[prev] = jnp.zeros(q_ref.shape[1:], BF16)

    @pl.when(s < XA_HEADS)
    def _():
        wq_all[head] = wq_ref[...].astype(BF16)

    @pl.when(s <= XA_HEADS)
    def _():
        wo_all[prev_head] = wo_ref[...].astype(BF16)

    @pl.when((lax.rem(s, XA_HEADS) == 0) & (s < n_tiles * XA_HEADS))
    def _():
        for rows in (pl.ds(r * NORM_ROWS, NORM_ROWS) for r in range(hn_ref.shape[0] // NORM_ROWS)):
            hn_ref[rows, :] = _rmsnorm(buf_ref[tile_slot, rows, :], g_ref[...]).astype(BF16)

    def step(project):
        q_prev = q_ref[prev]
        scores = _dot(q_prev, k_ref[...])
        if project:
            q_ref[cur] = _dot(hn_ref[...], wq_all[head]).astype(BF16)
        scores = scores * (q_prev.shape[-1] ** -0.5)
        p = jnp.exp(scores - jnp.max(scores, axis=-1, keepdims=True))
        p = p / jnp.sum(p, axis=-1, keepdims=True)
        o = _dot(p.astype(BF16), v_ref[...])
        o = jnp.where(s > 0, o, 0.0)
        buf_ref[prev_slot] += _dot(o.astype(BF16), wo_all[prev_head])

    @pl.when(s < n_tiles * XA_HEADS)
    def _():
        step(project=True)

    @pl.when(s == n_tiles * XA_HEADS)
    def _():
        step(project=False)
        store(n_tiles - 1).start(priority=ROW_TILE_DMA_THREAD)
        store(n_tiles - 1).wait()


def _xattn(h, g, w_q, k, v, w_o, *, seq, n_mem, tm):
    m, d = h.shape
    hdim = d // XA_HEADS
    tiles_per_seq = seq // tm
    n_tiles = m // tm
    n_steps = n_tiles * XA_HEADS
    assert m % tm == 0 and seq % tm == 0 and XA_HEADS >= 3

    def prev_tile(s):
        return jnp.maximum(s - 1, 0) // XA_HEADS

    def prev_head(s):
        return jnp.maximum(s - 1, 0) % XA_HEADS

    vmem = (2 * tm * d * 4 + tm * d * 2 + 2 * tm * hdim * 2
            + 2 * 2 * d * hdim * w_q.dtype.itemsize + 2 * 2 * n_mem * hdim * 2
            + 2 * d * d * 2)
    return pl.pallas_call(
        functools.partial(_xattn_kernel, n_tiles=n_tiles),
        grid=(n_steps + 1,),
        in_specs=[
            pl.BlockSpec(memory_space=pl.ANY),
            pl.BlockSpec((1, d), lambda s: (0, 0)),
            pl.BlockSpec((d, hdim), lambda s: (0, jnp.minimum(s, XA_HEADS - 1))),
            pl.BlockSpec((hdim, n_mem), lambda s: (prev_head(s), prev_tile(s) // tiles_per_seq)),
            pl.BlockSpec((n_mem, hdim), lambda s: (prev_tile(s) // tiles_per_seq, prev_head(s))),
            pl.BlockSpec((hdim, d), lambda s: (prev_head(jnp.minimum(s, XA_HEADS)), 0)),
        ],
        out_specs=pl.BlockSpec(memory_space=pl.ANY),
        out_shape=jax.ShapeDtypeStruct((m, d), F32),
        scratch_shapes=[pltpu.VMEM((2, tm, d), F32), pltpu.VMEM((tm, d), BF16),
                        pltpu.VMEM((2, tm, hdim), BF16),
                        pltpu.VMEM((XA_HEADS, d, hdim), BF16), pltpu.VMEM((XA_HEADS, hdim, d), BF16),
                        pltpu.SemaphoreType.DMA(()), pltpu.SemaphoreType.DMA(())],
        compiler_params=pltpu.CompilerParams(
            dimension_semantics=("arbitrary",), vmem_limit_bytes=_vmem_limit(vmem)),
        name="xattn",
    )(h, g, w_q, k, v, w_o)


def kernel(x, mem, g_ffn1, w_ffn1_in, w_ffn1_out, g_mix, w_mix_in, conv_w, conv_b, g_gm_v,
           w_spatial, b_spatial, w_mix_out, g_xattn, g_mem, w_xq, w_xk, w_xv, w_xo, g_ffn2,
           w_ffn2_in, w_ffn2_out, g_final):
    b, s, d = x.shape
    n_mem = mem.shape[1]
    depth = g_ffn1.shape[0]
    assert depth >= 1
    heads_per_step = MIX_COLS // GROUP_DIM
    ffn_tiles = dict(tm=2048, tf=256, row_chunk=1024)
    tm = 1024

    def row(v):
        return v.reshape(1, -1)

    h = x.reshape(b * s, d)
    mem2 = mem.reshape(b * n_mem, d)
    g_fin = row(g_final)
    for l in range(depth):
        last = l == depth - 1
        h = _ffn(h, row(g_ffn1[l]), w_ffn1_in[l], w_ffn1_out[l], g_fin,
                 final_norm=False, **ffn_tiles)
        b_s_t = b_spatial[l].reshape(-1, heads_per_step, CHUNK).transpose(0, 2, 1)
        h = _mix(h, row(g_mix[l]), w_mix_in[l], conv_w[l], row(conv_b[l]), row(g_gm_v[l]),
                 w_spatial[l], b_s_t, w_mix_out[l], seq=s, tm=tm)
        k, v = _kv(mem2, row(g_mem[l]), w_xk[l], w_xv[l], tn=512)
        h = _xattn(h, row(g_xattn[l]), w_xq[l], k, v, w_xo[l], seq=s, n_mem=n_mem, tm=tm)
        h = _ffn(h, row(g_ffn2[l]), w_ffn2_in[l], w_ffn2_out[l], g_fin,
                 final_norm=last, **ffn_tiles)
    return h.reshape(b, s, d)
```

```python
import functools

import jax
import jax.numpy as jnp
from jax import lax
from jax.experimental import pallas as pl
from jax.experimental.pallas import tpu as pltpu

GROUP_DIM = 128
CHUNK = 128
CONV_K = 3
XA_HEADS = 4
EPS = 1e-6

V7X_VMEM_BYTES = 64 * 1024 * 1024
SUBLANES = 8
MIX_COLS = 256
ROW_TILE_DMA_THREAD = 1
NORM_ROWS = 256

BF16 = jnp.bfloat16
F32 = jnp.float32


def _rmsnorm(x, g):
    y = x * lax.rsqrt(jnp.mean(x * x, axis=-1, keepdims=True) + EPS)
    return y * g


_dot = functools.partial(jnp.dot, preferred_element_type=F32)


TEMPORARIES_BYTES = 8 << 20
VMEM_RESERVE_BYTES = 4 << 20


def _vmem_limit(buffer_bytes):
    return min(buffer_bytes + buffer_bytes // 4 + TEMPORARIES_BYTES,
               V7X_VMEM_BYTES - VMEM_RESERVE_BYTES)


def _ffn_kernel(x_hbm, g_ref, win_hbm, wout_hbm, gf_ref, o_hbm, buf_ref, xn_ref, wg_ref, wu_ref,
                wo_ref, in_sem, out_sem, w_sem, *, final_norm, row_chunk):
    i = pl.program_id(0)
    n_tiles = pl.num_programs(0)
    tm, d = xn_ref.shape
    tf = wo_ref.shape[1]
    d_ff = wout_hbm.shape[0]
    nk = d_ff // tf
    slot = lax.rem(i, 2)
    other = 1 - slot

    def load(tile, dst):
        return pltpu.make_async_copy(x_hbm.at[pl.ds(tile * tm, tm)], buf_ref.at[dst], in_sem)

    def store(tile, src):
        return pltpu.make_async_copy(buf_ref.at[src], o_hbm.at[pl.ds(tile * tm, tm)], out_sem)

    def weight_copies(k, ws):
        cols = pl.ds(pl.multiple_of(k * tf, tf), tf)
        up_cols = pl.ds(pl.multiple_of(d_ff + k * tf, tf), tf)
        return (pltpu.make_async_copy(win_hbm.at[:, cols], wg_ref.at[ws], w_sem.at[0, ws]),
                pltpu.make_async_copy(win_hbm.at[:, up_cols], wu_ref.at[ws], w_sem.at[1, ws]),
                pltpu.make_async_copy(wout_hbm.at[cols, :], wo_ref.at[ws], w_sem.at[2, ws]))

    @pl.when(i == 0)
    def _():
        load(0, 0).start(priority=ROW_TILE_DMA_THREAD)
        for c in weight_copies(0, 0):
            c.start()

    load(i, slot).wait()
    for rows in (pl.ds(r * NORM_ROWS, NORM_ROWS) for r in range(tm // NORM_ROWS)):
        xn_ref[rows, :] = _rmsnorm(buf_ref[slot, rows, :], g_ref[...]).astype(BF16)

    @pl.when(i > 0)
    def _():
        store(i - 1, other).start(priority=ROW_TILE_DMA_THREAD)

    def block(k, carry):
        @pl.when(k == nk // 2)
        def _():
            @pl.when(i > 0)
            def _():
                store(i - 1, other).wait()

            @pl.when(i + 1 < n_tiles)
            def _():
                load(i + 1, other).start(priority=ROW_TILE_DMA_THREAD)

        ws = lax.rem(i * nk + k, 2)
        for c in weight_copies(k, ws):
            c.wait()

        for c in weight_copies(lax.rem(k + 1, nk), 1 - ws):
            c.start()

        w_gate, w_up, w_down = (r[ws].astype(BF16) for r in (wg_ref, wu_ref, wo_ref))
        for rows in (pl.ds(r * row_chunk, row_chunk) for r in range(tm // row_chunk)):
            xn = xn_ref[rows, :]
            gate = _dot(xn, w_gate)
            up = _dot(xn, w_up)
            hidden = (0.5 * gate * jax.nn.sigmoid(gate) * up).astype(BF16)
            buf_ref[slot, rows, :] += _dot(hidden, w_down)
        return carry

    lax.fori_loop(0, nk, block, 0)

    if final_norm:
        for rows in (pl.ds(r * NORM_ROWS, NORM_ROWS) for r in range(tm // NORM_ROWS)):
            buf_ref[slot, rows, :] = _rmsnorm(buf_ref[slot, rows, :], gf_ref[...])

    @pl.when(i == n_tiles - 1)
    def _():
        store(i, slot).start(priority=ROW_TILE_DMA_THREAD)
        for c in weight_copies(0, lax.rem(n_tiles * nk, 2)):
            c.wait()
        store(i, slot).wait()


def _ffn(x, g, w_in, w_out, g_final, *, final_norm, tm, tf, row_chunk):
    m, d = x.shape
    d_ff = w_out.shape[0]
    nk = d_ff // tf
    assert m % tm == 0 and d_ff % tf == 0 and nk >= 2 and tm % row_chunk == 0
    vmem = (2 * tm * d * 4
            + tm * d * 2
            + 2 * 3 * d * tf * w_in.dtype.itemsize)
    return pl.pallas_call(
        functools.partial(_ffn_kernel, final_norm=final_norm, row_chunk=row_chunk),
        grid=(m // tm,),
        in_specs=[
            pl.BlockSpec(memory_space=pl.ANY),
            pl.BlockSpec((1, d), lambda i: (0, 0)),
            pl.BlockSpec(memory_space=pl.ANY),
            pl.BlockSpec(memory_space=pl.ANY),
            pl.BlockSpec((1, d), lambda i: (0, 0)),
        ],
        out_specs=pl.BlockSpec(memory_space=pl.ANY),
        out_shape=jax.ShapeDtypeStruct((m, d), F32),
        scratch_shapes=[pltpu.VMEM((2, tm, d), F32), pltpu.VMEM((tm, d), BF16),
                        pltpu.VMEM((2, d, tf), w_in.dtype), pltpu.VMEM((2, d, tf), w_in.dtype),
                        pltpu.VMEM((2, tf, d), w_out.dtype),
                        pltpu.SemaphoreType.DMA(()), pltpu.SemaphoreType.DMA(()),
                        pltpu.SemaphoreType.DMA((3, 2))],
        compiler_params=pltpu.CompilerParams(
            dimension_semantics=("arbitrary",), vmem_limit_bytes=_vmem_limit(vmem)),
        name="ffn_final" if final_norm else "ffn",
    )(x, g, w_in, w_out, g_final)


def _stream_row_tiles(s, per, n_tiles, h_hbm, o_hbm, buf_ref, in_sem, out_sem):
    tm = buf_ref.shape[1]
    t = s // per
    j = lax.rem(s, per)

    def load(tile):
        return pltpu.make_async_copy(h_hbm.at[pl.ds(tile * tm, tm)],
                                     buf_ref.at[lax.rem(tile, 2)], in_sem)

    def store(tile):
        return pltpu.make_async_copy(buf_ref.at[lax.rem(tile, 2)],
                                     o_hbm.at[pl.ds(tile * tm, tm)], out_sem)

    @pl.when(s == 0)
    def _():
        load(0).start(priority=ROW_TILE_DMA_THREAD)

    @pl.when((j == 0) & (t < n_tiles))
    def _():
        load(t).wait()

    @pl.when((j == 1) & (t >= 1))
    def _():
        store(t - 1).start(priority=ROW_TILE_DMA_THREAD)

    @pl.when(j == 2)
    def _():
        @pl.when(t >= 1)
        def _():
            store(t - 1).wait()

        @pl.when(t + 1 < n_tiles)
        def _():
            load(t + 1).start(priority=ROW_TILE_DMA_THREAD)

    return store


def _mix_kernel(h_hbm, g_ref, wb_ref, wc_ref, wh_ref, wu_ref, wv_ref, cw_ref, cb_ref,
                gv_ref, ws_ref, bs_ref, woc_ref, wog_ref, o_hbm, buf_ref, hn_ref, y_ref, carry_ref,
                in_sem, out_sem, *, nj, n_tiles, tiles_per_seq):
    s = pl.program_id(0)
    j = lax.rem(s, nj)
    cur = lax.rem(s, 2)
    prev = 1 - cur
    tm = hn_ref.shape[0]
    tile_slot = lax.rem(s // nj, 2)
    prev_slot = lax.rem(jnp.maximum(s - 1, 0) // nj, 2)
    store = _stream_row_tiles(s, nj, n_tiles, h_hbm, o_hbm, buf_ref, in_sem, out_sem)

    @pl.when(s == 0)
    def _():
        y_ref[prev] = jnp.zeros(y_ref.shape[1:], BF16)

    @pl.when((j == 0) & (s < n_tiles * nj))
    def _():
        hn_ref[...] = _rmsnorm(buf_ref[tile_slot], g_ref[...]).astype(BF16)

    @pl.when(lax.rem(s // nj, tiles_per_seq) == 0)
    def _():
        carry_ref[j] = jnp.zeros((SUBLANES, MIX_COLS), F32)

    def step(compute_block):
        if compute_block:
            w_in = jnp.concatenate(
                [r[...].astype(BF16) for r in (wv_ref, wu_ref, wc_ref, wh_ref, wb_ref)], axis=1)
            proj = _dot(hn_ref[...], w_in)
            v, u, gate_c, h_c, gate_b = (proj[:, n * MIX_COLS:(n + 1) * MIX_COLS]
                                         for n in range(5))

        y_prev = y_ref[prev]
        buf_ref[prev_slot] += _dot(y_prev[:, :MIX_COLS], woc_ref[...].astype(BF16))

        if compute_block:
            gv = gv_ref[...]
            bs = bs_ref[...]
            tri = (lax.broadcasted_iota(jnp.int32, (CHUNK, CHUNK), 0)
                   >= lax.broadcasted_iota(jnp.int32, (CHUNK, CHUNK), 1))
            sgs = []
            for hh in range(MIX_COLS // GROUP_DIM):
                lanes = slice(hh * GROUP_DIM, (hh + 1) * GROUP_DIM)
                vh = _rmsnorm(v[:, lanes], gv[:, lanes]).astype(BF16)
                w = jnp.where(tri, ws_ref[hh], 0.0).astype(BF16)
                sgs.append([_dot(w, vh[c * CHUNK:(c + 1) * CHUNK]) for c in range(tm // CHUNK)])

        buf_ref[prev_slot] += _dot(y_prev[:, MIX_COLS:], wog_ref[...].astype(BF16))

        if not compute_block:
            return
        for hh, per_chunk in enumerate(sgs):
            lanes = slice(hh * GROUP_DIM, (hh + 1) * GROUP_DIM)
            for c, sg in enumerate(per_chunk):
                rows = slice(c * CHUNK, (c + 1) * CHUNK)
                y_ref[cur, rows, MIX_COLS + hh * GROUP_DIM:MIX_COLS + (hh + 1) * GROUP_DIM] = (
                    u[rows, lanes] * (sg + bs[:, hh:hh + 1])).astype(BF16)

        z = gate_c * h_c
        tail = carry_ref[j]
        carry_ref[j] = z[tm - SUBLANES:, :]
        cw = cw_ref[...]
        cb = cb_ref[...]
        row = lax.broadcasted_iota(jnp.int32, (SUBLANES, MIX_COLS), 0)
        z1 = pltpu.roll(z, 1, 0)
        z2 = pltpu.roll(z, 2, 0)
        head1 = jnp.where(row < 1, pltpu.roll(tail, 1, 0), z1[:SUBLANES])
        head2 = jnp.where(row < 2, pltpu.roll(tail, 2, 0), z2[:SUBLANES])
        z1 = jnp.concatenate([head1, z1[SUBLANES:]], axis=0)
        z2 = jnp.concatenate([head2, z2[SUBLANES:]], axis=0)
        y_ref[cur, :, :MIX_COLS] = (
            gate_b * (cb + cw[2:3] * z + cw[1:2] * z1 + cw[0:1] * z2)).astype(BF16)

    @pl.when(s < n_tiles * nj)
    def _():
        step(compute_block=True)

    @pl.when(s == n_tiles * nj)
    def _():
        step(compute_block=False)
        store(n_tiles - 1).start(priority=ROW_TILE_DMA_THREAD)
        store(n_tiles - 1).wait()


def _mix(h, g, w_in, conv_w, conv_b, g_v, w_s, b_s_t, w_out, *, seq, tm):
    m, d = h.shape
    width = w_out.shape[0]
    conv_width = width // 2
    nj = conv_width // MIX_COLS
    heads_per_step = MIX_COLS // GROUP_DIM
    n_tiles = m // tm
    n_steps = n_tiles * nj
    assert m % tm == 0 and seq % tm == 0 and tm % CHUNK == 0 and nj >= 3

    def prev_j(s):
        return jnp.maximum(s - 1, 0) % nj

    def col_block(offset):
        return pl.BlockSpec((d, MIX_COLS), lambda s: (0, offset + s % nj))

    vmem = (2 * tm * d * 4 + tm * d * 2 + 2 * tm * 2 * MIX_COLS * 2
            + 2 * 5 * d * MIX_COLS * w_in.dtype.itemsize
            + 2 * 2 * MIX_COLS * d * w_out.dtype.itemsize)
    return pl.pallas_call(
        functools.partial(_mix_kernel, nj=nj, n_tiles=n_tiles, tiles_per_seq=seq // tm),
        grid=(n_steps + 1,),
        in_specs=[
            pl.BlockSpec(memory_space=pl.ANY),
            pl.BlockSpec((1, d), lambda s: (0, 0)),
            col_block(0), col_block(nj), col_block(2 * nj),
            col_block(3 * nj), col_block(4 * nj),
            pl.BlockSpec((CONV_K, MIX_COLS), lambda s: (0, s % nj)),
            pl.BlockSpec((1, MIX_COLS), lambda s: (0, s % nj)),
            pl.BlockSpec((1, MIX_COLS), lambda s: (0, s % nj)),
            pl.BlockSpec((heads_per_step, CHUNK, CHUNK), lambda s: (s % nj, 0, 0)),
            pl.BlockSpec((None, CHUNK, heads_per_step), lambda s: (s % nj, 0, 0)),
            pl.BlockSpec((MIX_COLS, d), lambda s: (prev_j(s), 0)),
            pl.BlockSpec((MIX_COLS, d), lambda s: (nj + prev_j(s), 0)),
        ],
        out_specs=pl.BlockSpec(memory_space=pl.ANY),
        out_shape=jax.ShapeDtypeStruct((m, d), F32),
        scratch_shapes=[pltpu.VMEM((2, tm, d), F32), pltpu.VMEM((tm, d), BF16),
                        pltpu.VMEM((2, tm, 2 * MIX_COLS), BF16),
                        pltpu.VMEM((nj, SUBLANES, MIX_COLS), F32),
                        pltpu.SemaphoreType.DMA(()), pltpu.SemaphoreType.DMA(())],
        compiler_params=pltpu.CompilerParams(
            dimension_semantics=("arbitrary",), vmem_limit_bytes=_vmem_limit(vmem)),
        name="mix",
    )(h, g, w_in, w_in, w_in, w_in, w_in, conv_w, conv_b, g_v, w_s, b_s_t, w_out, w_out)


def _kv_kernel(mem_ref, g_ref, wk_ref, wv_ref, k_ref, v_ref, mn_ref):
    @pl.when(pl.program_id(0) == 0)
    def _():
        mn_ref[...] = _rmsnorm(mem_ref[...], g_ref[...]).astype(BF16)

    mn = mn_ref[...]
    tn = wk_ref.shape[1]
    kv = _dot(mn, jnp.concatenate([wk_ref[...].astype(BF16), wv_ref[...].astype(BF16)], axis=1))
    k_ref[...] = kv[:, :tn].T.astype(BF16)
    v_ref[...] = kv[:, tn:].astype(BF16)


def _kv(mem, g, w_k, w_v, *, tn):
    m, d = mem.shape
    vmem = (2 * m * d * 4 + m * d * 2 + 2 * 2 * d * tn * w_k.dtype.itemsize
            + 2 * 2 * m * tn * 2)
    return pl.pallas_call(
        _kv_kernel,
        grid=(d // tn,),
        in_specs=[
            pl.BlockSpec((m, d), lambda n: (0, 0)),
            pl.BlockSpec((1, d), lambda n: (0, 0)),
            pl.BlockSpec((d, tn), lambda n: (0, n)),
            pl.BlockSpec((d, tn), lambda n: (0, n)),
        ],
        out_specs=[pl.BlockSpec((tn, m), lambda n: (n, 0)), pl.BlockSpec((m, tn), lambda n: (0, n))],
        out_shape=[jax.ShapeDtypeStruct((d, m), BF16), jax.ShapeDtypeStruct((m, d), BF16)],
        scratch_shapes=[pltpu.VMEM((m, d), BF16)],
        compiler_params=pltpu.CompilerParams(
            dimension_semantics=("arbitrary",), vmem_limit_bytes=_vmem_limit(vmem)),
        name="kv",
    )(mem, g, w_k, w_v)


def _xattn_kernel(h_hbm, g_ref, wq_ref, k_ref, v_ref, wo_ref, o_hbm, buf_ref, hn_ref, q_ref,
                  wq_all, wo_all, in_sem, out_sem, *, n_tiles):
    s = pl.program_id(0)
    cur = lax.rem(s, 2)
    prev = 1 - cur
    head = lax.rem(s, XA_HEADS)
    prev_head = lax.rem(jnp.maximum(s - 1, 0), XA_HEADS)
    tile_slot = lax.rem(s // XA_HEADS, 2)
    prev_slot = lax.rem(jnp.maximum(s - 1, 0) // XA_HEADS, 2)
    store = _stream_row_tiles(s, XA_HEADS, n_tiles, h_hbm, o_hbm, buf_ref, in_sem, out_sem)

    @pl.when(s == 0)
    def _():
        q_ref[prev] = jnp.zeros(q_ref.shape[1:], BF16)

    @pl.when(s < XA_HEADS)
    def _():
        wq_all[head] = wq_ref[...].astype(BF16)

    @pl.when(s <= XA_HEADS)
    def _():
        wo_all[prev_head] = wo_ref[...].astype(BF16)

    @pl.when((lax.rem(s, XA_HEADS) == 0) & (s < n_tiles * XA_HEADS))
    def _():
        hn_ref[...] = _rmsnorm(buf_ref[tile_slot], g_ref[...]).astype(BF16)

    def step(project):
        q_prev = q_ref[prev]
        scores = _dot(q_prev, k_ref[...])
        if project:
            q_ref[cur] = _dot(hn_ref[...], wq_all[head]).astype(BF16)
        scores = scores * (q_prev.shape[-1] ** -0.5)
        p = jnp.exp(scores - jnp.max(scores, axis=-1, keepdims=True))
        p = p / jnp.sum(p, axis=-1, keepdims=True)
        o = _dot(p.astype(BF16), v_ref[...])
        o = jnp.where(s > 0, o, 0.0)
        buf_ref[prev_slot] += _dot(o.astype(BF16), wo_all[prev_head])

    @pl.when(s < n_tiles * XA_HEADS)
    def _():
        step(project=True)

    @pl.when(s == n_tiles * XA_HEADS)
    def _():
        step(project=False)
        store(n_tiles - 1).start(priority=ROW_TILE_DMA_THREAD)
        store(n_tiles - 1).wait()


def _xattn(h, g, w_q, k, v, w_o, *, seq, n_mem, tm):
    m, d = h.shape
    hdim = d // XA_HEADS
    tiles_per_seq = seq // tm
    n_tiles = m // tm
    n_steps = n_tiles * XA_HEADS
    assert m % tm == 0 and seq % tm == 0 and XA_HEADS >= 3

    def prev_tile(s):
        return jnp.maximum(s - 1, 0) // XA_HEADS

    def prev_head(s):
        return jnp.maximum(s - 1, 0) % XA_HEADS

    vmem = (2 * tm * d * 4 + tm * d * 2 + 2 * tm * hdim * 2
            + 2 * 2 * d * hdim * w_q.dtype.itemsize + 2 * 2 * n_mem * hdim * 2
            + 2 * d * d * 2)
    return pl.pallas_call(
        functools.partial(_xattn_kernel, n_tiles=n_tiles),
        grid=(n_steps + 1,),
        in_specs=[
            pl.BlockSpec(memory_space=pl.ANY),
            pl.BlockSpec((1, d), lambda s: (0, 0)),
            pl.BlockSpec((d, hdim), lambda s: (0, jnp.minimum(s, XA_HEADS - 1))),
            pl.BlockSpec((hdim, n_mem), lambda s: (prev_head(s), prev_tile(s) // tiles_per_seq)),
            pl.BlockSpec((n_mem, hdim), lambda s: (prev_tile(s) // tiles_per_seq, prev_head(s))),
            pl.BlockSpec((hdim, d), lambda s: (prev_head(jnp.minimum(s, XA_HEADS)), 0)),
        ],
        out_specs=pl.BlockSpec(memory_space=pl.ANY),
        out_shape=jax.ShapeDtypeStruct((m, d), F32),
        scratch_shapes=[pltpu.VMEM((2, tm, d), F32), pltpu.VMEM((tm, d), BF16),
                        pltpu.VMEM((2, tm, hdim), BF16),
                        pltpu.VMEM((XA_HEADS, d, hdim), BF16), pltpu.VMEM((XA_HEADS, hdim, d), BF16),
                        pltpu.SemaphoreType.DMA(()), pltpu.SemaphoreType.DMA(())],
        compiler_params=pltpu.CompilerParams(
            dimension_semantics=("arbitrary",), vmem_limit_bytes=_vmem_limit(vmem)),
        name="xattn",
    )(h, g, w_q, k, v, w_o)


def kernel(x, mem, g_ffn1, w_ffn1_in, w_ffn1_out, g_mix, w_mix_in, conv_w, conv_b, g_gm_v,
           w_spatial, b_spatial, w_mix_out, g_xattn, g_mem, w_xq, w_xk, w_xv, w_xo, g_ffn2,
           w_ffn2_in, w_ffn2_out, g_final):
    b, s, d = x.shape
    n_mem = mem.shape[1]
    depth = g_ffn1.shape[0]
    assert depth >= 1
    heads_per_step = MIX_COLS // GROUP_DIM
    ffn_tiles = dict(tm=2048, tf=256, row_chunk=1024)
    tm = 1024

    def row(v):
        return v.reshape(1, -1)

    h = x.reshape(b * s, d)
    mem2 = mem.reshape(b * n_mem, d)
    g_fin = row(g_final)
    for l in range(depth):
        last = l == depth - 1
        h = _ffn(h, row(g_ffn1[l]), w_ffn1_in[l], w_ffn1_out[l], g_fin,
                 final_norm=False, **ffn_tiles)
        b_s_t = b_spatial[l].reshape(-1, heads_per_step, CHUNK).transpose(0, 2, 1)
        h = _mix(h, row(g_mix[l]), w_mix_in[l], conv_w[l], row(conv_b[l]), row(g_gm_v[l]),
                 w_spatial[l], b_s_t, w_mix_out[l], seq=s, tm=tm)
        k, v = _kv(mem2, row(g_mem[l]), w_xk[l], w_xv[l], tn=512)
        h = _xattn(h, row(g_xattn[l]), w_xq[l], k, v, w_xo[l], seq=s, n_mem=n_mem, tm=tm)
        h = _ffn(h, row(g_ffn2[l]), w_ffn2_in[l], w_ffn2_out[l], g_fin,
                 final_norm=last, **ffn_tiles)
    return h.reshape(b, s, d)
```
